```python
import math
import jax, jax.numpy as jnp
from jax import lax
import numpy as np

D_MODEL = 2048
BATCH = 1
SEQ = 16384
DEPTH = 1
DEC_BATCH = 32
DEC_SEQ = 64
PAST_LEN = 1024

CHUNK = 64
HD_A = 128
H_A = D_MODEL // HD_A
D_A = H_A * HD_A
H_I = 16
D_IDX = 64
TOPK_MAX = 256
Q_BLOCK = 128
NUM_BUCKETS = 32
MAX_DISTANCE = 128
DK_B = 128
DV_B = 128
H_B = D_MODEL // DV_B
D_B = H_B * DV_B
CONVB_DIM = 2 * H_B * DK_B + D_B
CONVB_WIDTH = 4
D_FF = 5632
FFN_CONV_WIDTH = 3
EPS = 1e-6
IN_SPLITS = (D_A, D_A, D_A, H_I * D_IDX, D_IDX, H_I, CONVB_DIM, H_B, H_B, D_B, D_MODEL, D_MODEL)
D_IN = 3 * D_A + H_I * D_IDX + D_IDX + H_I + CONVB_DIM + 2 * H_B + D_B + 2 * D_MODEL

kernel_name = 'hybrid_dsa_gdn_convffn_stream_step'


def rmsnorm(x, g):
    xf = x.astype(jnp.float32)
    y = xf * lax.rsqrt(jnp.mean(xf * xf, axis=-1, keepdims=True) + EPS)
    return y.astype(x.dtype) * g


def l2norm(x):
    xf = x.astype(jnp.float32)
    return (xf * lax.rsqrt(jnp.sum(xf * xf, axis=-1, keepdims=True) + EPS)).astype(x.dtype)


def causal_dwconv(x, state, w):
    width = w.shape[0]
    L = x.shape[1]
    xp = jnp.concatenate([state.astype(x.dtype), x], axis=1)
    y = sum(xp[:, i:i + L] * w[i] for i in range(width))
    return y, xp[:, xp.shape[1] - (width - 1):]


def t5_bucket(rel):
    half = NUM_BUCKETS // 2
    max_exact = half // 2
    ret = jnp.where(rel > 0, half, 0)
    n = jnp.abs(rel)
    nf = jnp.maximum(n, 1).astype(jnp.float32)
    large = max_exact + (jnp.log(nf / max_exact) / math.log(MAX_DISTANCE / max_exact)
                         * (half - max_exact)).astype(jnp.int32)
    large = jnp.minimum(large, half - 1)
    return ret + jnp.where(n < max_exact, n, large)


def sparse_attention(q, k, v, qi, ki, wi, q_pos, k_pos, rel_bias):
    B, Q, H, dh = q.shape
    L = k.shape[1]
    n_sel = min(TOPK_MAX, L // 4)
    qb = min(Q_BLOCK, Q)
    limit = (q_pos // CHUNK + 1) * CHUNK

    def block(i):
        s = i * qb
        q_b = lax.dynamic_slice_in_dim(q, s, qb, axis=1)
        qi_b = lax.dynamic_slice_in_dim(qi, s, qb, axis=1)
        wi_b = lax.dynamic_slice_in_dim(wi, s, qb, axis=1)
        pos_b = lax.dynamic_slice_in_dim(q_pos, s, qb)
        lim_b = lax.dynamic_slice_in_dim(limit, s, qb)
        dots = jnp.einsum('bqhd,bsd->bqhs', qi_b, ki).astype(jnp.float32)
        score = jnp.einsum('bqhs,bqh->bqs', jax.nn.relu(dots), wi_b.astype(jnp.float32))
        score = jnp.where(k_pos[None, None, :] < lim_b[None, :, None], score, -jnp.inf)
        _, sel = lax.top_k(score, n_sel)
        k_sel = jax.vmap(lambda kk, ii: kk[ii])(k, sel)
        v_sel = jax.vmap(lambda vv, ii: vv[ii])(v, sel)
        pos_sel = k_pos[sel]
        logits = jnp.einsum('bqhd,bqkhd->bqhk', q_b, k_sel).astype(jnp.float32) * (dh ** -0.5)
        bias = jnp.moveaxis(rel_bias[t5_bucket(pos_sel - pos_b[None, :, None])], 3, 2).astype(jnp.float32)
        ok = (pos_sel < lim_b[None, :, None])[:, :, None, :]
        p = jax.nn.softmax(jnp.where(ok, logits + bias, -jnp.inf), axis=-1)
        return jnp.einsum('bqhk,bqkhd->bqhd', p.astype(v.dtype), v_sel)

    out = lax.map(block, jnp.arange(Q // qb))
    return jnp.moveaxis(out, 0, 1).reshape(B, Q, H, dh)


def gated_delta_rule(q, k, v, g, beta, s0):
    B, L, H, dk = q.shape
    dv = v.shape[-1]
    out_dtype = v.dtype
    f32 = jnp.float32
    pad = (-L) % CHUNK
    n = (L + pad) // CHUNK

    def blocks(t):
        t = jnp.pad(t.astype(f32), [(0, 0), (0, pad)] + [(0, 0)] * (t.ndim - 2))
        t = t.reshape((B, n, CHUNK) + t.shape[2:])
        return jnp.moveaxis(t, 3, 2)

    q, k, v, g, beta = blocks(q), blocks(k), blocks(v), blocks(g), blocks(beta)
    gc = jnp.cumsum(g, axis=-1)
    idx = jnp.arange(CHUNK)
    incl = idx[:, None] >= idx[None, :]
    strict = idx[:, None] > idx[None, :]
    decay = jnp.exp(jnp.where(incl, gc[..., :, None] - gc[..., None, :], -jnp.inf))
    kb = k * beta[..., None]
    vb = v * beta[..., None]
    m = jnp.where(strict, jnp.einsum('bnhid,bnhjd->bnhij', kb, k) * decay, 0.0)
    eye = jnp.eye(CHUNK, dtype=f32)
    tinv = lax.linalg.triangular_solve(m + eye, jnp.broadcast_to(eye, m.shape),
                                       left_side=True, lower=True, unit_diagonal=True)
    u = tinv @ vb
    w = tinv @ (kb * jnp.exp(gc)[..., None])
    attn = jnp.where(incl, jnp.einsum('bnhid,bnhjd->bnhij', q, k) * decay, 0.0)
    qg = q * jnp.exp(gc)[..., None]
    kd = k * jnp.exp(gc[..., -1:] - gc)[..., None]
    glast = jnp.exp(gc[..., -1])

    def step(s, xs):
        u_n, w_n, a_n, qg_n, kd_n, gl_n = xs
        v_new = u_n - jnp.einsum('bhck,bhkv->bhcv', w_n, s)
        o = jnp.einsum('bhck,bhkv->bhcv', qg_n, s) + jnp.einsum('bhij,bhjv->bhiv', a_n, v_new)
        s = s * gl_n[..., None, None] + jnp.einsum('bhck,bhcv->bhkv', kd_n, v_new)
        return s, o

    xs = tuple(jnp.moveaxis(t, 1, 0) for t in (u, w, attn, qg, kd, glast))
    s_fin, o = lax.scan(step, s0.astype(f32), xs)
    o = jnp.moveaxis(jnp.moveaxis(o, 0, 1), 2, 3).reshape(B, n * CHUNK, H, dv)[:, :L]
    return o.astype(out_dtype), s_fin.astype(s0.dtype)


def hybrid_layer(x, c, cache_k, cache_v, cache_kidx, st_conv_b, st_delta, st_ffn, rel_bias,
                 w_ada, b_ada, g_norm1, w_in, conv_b_w, a_log, dt_bias, g_delta_norm, w_out,
                 g_norm2, w_ffn_gate, ffn_conv_w, ffn_conv_b, w_ffn_up, w_ffn_down):
    B, T, _ = x.shape
    P = cache_k.shape[1]
    mod = jnp.einsum('bd,de->be', jax.nn.silu(c), w_ada) + b_ada
    sh1, sc1, gt1, sh2, sc2, gt2 = [m[:, None, :] for m in jnp.split(mod, 6, axis=-1)]

    h = rmsnorm(x, g_norm1) * (1 + sc1) + sh1
    proj = jnp.einsum('btd,de->bte', h, w_in)
    offsets = [int(o) for o in np.cumsum(IN_SPLITS)[:-1]]
    qa, ka, va, qi, ki, wi, qkv_b, a_b, b_b, z_b, gate_a, gate_b = jnp.split(proj, offsets, axis=-1)

    qa = qa.reshape(B, T, H_A, HD_A)
    ka = ka.reshape(B, T, H_A, HD_A)
    va = va.reshape(B, T, H_A, HD_A)
    qi = qi.reshape(B, T, H_I, D_IDX)
    wi = wi * ((H_I * D_IDX) ** -0.5)
    k_all = jnp.concatenate([cache_k.astype(ka.dtype), ka], axis=1)
    v_all = jnp.concatenate([cache_v.astype(va.dtype), va], axis=1)
    ki_all = jnp.concatenate([cache_kidx.astype(ki.dtype), ki], axis=1)
    q_pos = P + jnp.arange(T, dtype=jnp.int32)
    k_pos = jnp.arange(P + T, dtype=jnp.int32)
    o_a = sparse_attention(qa, k_all, v_all, qi, ki_all, wi, q_pos, k_pos, rel_bias).reshape(B, T, D_A)

    conv_qkv, new_conv_b = causal_dwconv(qkv_b, st_conv_b, conv_b_w)
    conv_qkv = jax.nn.silu(conv_qkv)
    q_b, k_b, v_b = jnp.split(conv_qkv, [H_B * DK_B, 2 * H_B * DK_B], axis=-1)
    q_b = l2norm(q_b.reshape(B, T, H_B, DK_B)) * (DK_B ** -0.5)
    k_b = l2norm(k_b.reshape(B, T, H_B, DK_B))
    v_b = v_b.reshape(B, T, H_B, DV_B)
    log_decay = -jnp.exp(a_log.astype(jnp.float32)) * jax.nn.softplus(a_b.astype(jnp.float32) + dt_bias.astype(jnp.float32))
    beta = jax.nn.sigmoid(b_b.astype(jnp.float32))
    o_b, new_delta = gated_delta_rule(q_b, k_b, v_b, log_decay, beta, st_delta)
    o_b = (rmsnorm(o_b, g_delta_norm) * jax.nn.silu(z_b.reshape(B, T, H_B, DV_B))).reshape(B, T, D_B)

    mixed = jax.nn.sigmoid(gate_a) * o_a + jax.nn.sigmoid(gate_b) * o_b
    x = x + gt1 * jnp.einsum('btd,de->bte', mixed, w_out)

    h2 = rmsnorm(x, g_norm2) * (1 + sc2) + sh2
    a, new_ffn = causal_dwconv(jnp.einsum('btd,df->btf', h2, w_ffn_gate), st_ffn, ffn_conv_w)
    y = jnp.einsum('btf,fd->btd', jax.nn.silu(a + ffn_conv_b) * jnp.einsum('btd,df->btf', h2, w_ffn_up), w_ffn_down)
    x = x + gt2 * y
    return x, ka, va, ki, new_conv_b, new_delta, new_ffn


def run_group(x, c, cache_k, cache_v, cache_kidx, st_conv_b, st_delta, st_ffn, rel_bias, g_final, layer_weights):
    outs = [[] for _ in range(6)]
    for l in range(DEPTH):
        x, ka, va, ki, cb, sd, sf = hybrid_layer(
            x, c, cache_k[l], cache_v[l], cache_kidx[l], st_conv_b[l], st_delta[l], st_ffn[l], rel_bias,
            *[w[l] for w in layer_weights])
        for lst, val in zip(outs, (ka, va, ki, cb, sd, sf)):
            lst.append(val)
    y = rmsnorm(x, g_final)
    st = [jnp.stack(lst, axis=0) for lst in outs]
    return y, st[0], st[1], st[2], st[3], st[4], st[5]


def setup_inputs(seed: int = 0) -> dict:
    key = jax.random.key(seed)
    ks = jax.random.split(key, 28)

    def nrm(k, shape, s):
        return jax.random.normal(k, shape, jnp.float32) * s

    d = DEPTH
    dt = jnp.exp(jax.random.uniform(ks[15], (d, H_B), minval=math.log(1e-3), maxval=math.log(1e-1)))
    return {
        'x_prompt': nrm(ks[0], (BATCH, SEQ, D_MODEL), 1.0),
        'x_sample': nrm(ks[1], (DEC_BATCH, DEC_SEQ, D_MODEL), 1.0),
        'c_prompt': nrm(ks[2], (BATCH, D_MODEL), 1.0),
        'c_sample': nrm(ks[3], (DEC_BATCH, D_MODEL), 1.0),
        'cache_k': nrm(ks[4], (d, DEC_BATCH, PAST_LEN, H_A, HD_A), 1.0),
        'cache_v': nrm(ks[5], (d, DEC_BATCH, PAST_LEN, H_A, HD_A), 1.0),
        'cache_kidx': nrm(ks[6], (d, DEC_BATCH, PAST_LEN, D_IDX), 1.0),
        'state_conv_b': nrm(ks[7], (d, DEC_BATCH, CONVB_WIDTH - 1, CONVB_DIM), 1.0),
        'state_delta': nrm(ks[8], (d, DEC_BATCH, H_B, DK_B, DV_B), 0.1),
        'state_ffn_conv': nrm(ks[9], (d, DEC_BATCH, FFN_CONV_WIDTH - 1, D_FF), 1.0),
        'rel_bias': nrm(ks[10], (NUM_BUCKETS, H_A), 0.5),
        'g_final': 1.0 + nrm(ks[11], (D_MODEL,), 0.02),
        'w_ada': nrm(ks[12], (d, D_MODEL, 6 * D_MODEL), 0.5 * D_MODEL ** -0.5),
        'b_ada': nrm(ks[13], (d, 6 * D_MODEL), 0.01),
        'g_norm1': 1.0 + nrm(ks[14], (d, D_MODEL), 0.02),
        'w_in': nrm(ks[16], (d, D_MODEL, D_IN), D_MODEL ** -0.5),
        'conv_b_w': nrm(ks[17], (d, CONVB_WIDTH, CONVB_DIM), CONVB_WIDTH ** -0.5),
        'a_log': jnp.log(jax.random.uniform(ks[18], (d, H_B), minval=1.0, maxval=16.0)),
        'dt_bias': dt + jnp.log(-jnp.expm1(-dt)),
        'g_delta_norm': 1.0 + nrm(ks[19], (d, DV_B), 0.02),
        'w_out': nrm(ks[20], (d, D_A, D_MODEL), D_A ** -0.5),
        'g_norm2': 1.0 + nrm(ks[21], (d, D_MODEL), 0.02),
        'w_ffn_gate': nrm(ks[22], (d, D_MODEL, D_FF), D_MODEL ** -0.5),
        'ffn_conv_w': nrm(ks[23], (d, FFN_CONV_WIDTH, D_FF), FFN_CONV_WIDTH ** -0.5),
        'ffn_conv_b': nrm(ks[24], (d, D_FF), 0.01),
        'w_ffn_up': nrm(ks[25], (d, D_MODEL, D_FF), D_MODEL ** -0.5),
        'w_ffn_down': nrm(ks[26], (d, D_FF, D_MODEL), D_FF ** -0.5),
    }


def reference(x_prompt, x_sample, c_prompt, c_sample, cache_k, cache_v, cache_kidx, state_conv_b,
              state_delta, state_ffn_conv, rel_bias, g_final, w_ada, b_ada, g_norm1, w_in, conv_b_w,
              a_log, dt_bias, g_delta_norm, w_out, g_norm2, w_ffn_gate, ffn_conv_w, ffn_conv_b,
              w_ffn_up, w_ffn_down):
    layer_weights = (w_ada, b_ada, g_norm1, w_in, conv_b_w, a_log, dt_bias, g_delta_norm, w_out,
                     g_norm2, w_ffn_gate, ffn_conv_w, ffn_conv_b, w_ffn_up, w_ffn_down)
    bp = x_prompt.shape[0]
    dt = x_prompt.dtype
    y_prompt, k_p, v_p, ki_p, cb_p, sd_p, sf_p = run_group(
        x_prompt, c_prompt,
        jnp.zeros((DEPTH, bp, 0, H_A, HD_A), dt), jnp.zeros((DEPTH, bp, 0, H_A, HD_A), dt),
        jnp.zeros((DEPTH, bp, 0, D_IDX), dt), jnp.zeros((DEPTH, bp, CONVB_WIDTH - 1, CONVB_DIM), dt),
        jnp.zeros((DEPTH, bp, H_B, DK_B, DV_B), dt), jnp.zeros((DEPTH, bp, FFN_CONV_WIDTH - 1, D_FF), dt),
        rel_bias, g_final, layer_weights)
    y_sample, k_s, v_s, ki_s, cb_s, sd_s, sf_s = run_group(
        x_sample, c_sample, cache_k, cache_v, cache_kidx, state_conv_b, state_delta, state_ffn_conv,
        rel_bias, g_final, layer_weights)
    return (y_prompt, y_sample, k_p, v_p, ki_p, cb_p, sd_p, sf_p, k_s, v_s, ki_s, cb_s, sd_s, sf_s)
```

```python
import functools
import math

import jax
import jax.numpy as jnp
from jax import lax
from jax.experimental import pallas as pl
from jax.experimental.pallas import tpu as pltpu

F32 = jnp.float32
BF16 = jnp.bfloat16
I32 = jnp.int32

CHUNK = 64
HD_A = 128
H_I = 16
D_IDX = 64
TOPK_MAX = 256
NUM_BUCKETS = 32
MAX_DISTANCE = 128
DK_B = 128
DV_B = 128
CONVB_WIDTH = 4
FFN_CONV_WIDTH = 3
EPS = 1e-6
LOG2E = math.log2(math.e)

V7X_VMEM_BYTES = 64 * 1024 * 1024
LANES = 128
SUBLANES = 8
INT_MIN = -(2 ** 31)


def _cparams(sem, vmem_mb):
    assert vmem_mb * 1024 * 1024 < V7X_VMEM_BYTES
    return pltpu.CompilerParams(dimension_semantics=sem, vmem_limit_bytes=vmem_mb * 1024 * 1024)


def _sigmoid(x):
    return 1.0 / (1.0 + jnp.exp(-x))


def _silu(x):
    return x * _sigmoid(x)


def _dot(a, b, dims=None, passes=1):
    dn = dims if dims is not None else (((a.ndim - 1,), (0,)), ((), ()))
    if passes == 6:
        return lax.dot_general(a, b, dn, precision=lax.Precision.HIGHEST, preferred_element_type=F32)
    ah = a.astype(BF16)
    bh = b.astype(BF16)
    out = lax.dot_general(ah, bh, dn, preferred_element_type=F32)
    if passes == 3:
        al = (a - ah.astype(F32)).astype(BF16)
        bl = (b - bh.astype(F32)).astype(BF16)
        out = out + lax.dot_general(ah, bl, dn, preferred_element_type=F32)
        out = out + lax.dot_general(al, bh, dn, preferred_element_type=F32)
    return out


_NT = (((1,), (1,)), ((), ()))
_TN = (((0,), (0,)), ((), ()))


def _adaln_kernel(c_ref, w_ref, b_ref, o_ref):
    o_ref[...] = _dot(_silu(c_ref[...]), w_ref[...]) + b_ref[...]


def _adaln(c, w, b):
    bp, d = c.shape
    n = w.shape[1]
    tn = 1024
    return pl.pallas_call(
        _adaln_kernel,
        out_shape=jax.ShapeDtypeStruct((bp, n), F32),
        grid=(n // tn,),
        in_specs=[pl.BlockSpec((bp, d), lambda j: (0, 0)),
                  pl.BlockSpec((d, tn), lambda j: (0, j)),
                  pl.BlockSpec((1, tn), lambda j: (0, j))],
        out_specs=pl.BlockSpec((bp, tn), lambda j: (0, j)),
        compiler_params=_cparams(("arbitrary",), 40),
        name="adaln",
    )(c, w, b)


def _norm_mod_kernel(x_ref, g_ref, sc_ref, sh_ref, o_ref):
    x = x_ref[...]
    y = x * lax.rsqrt(jnp.mean(x * x, axis=-1, keepdims=True) + EPS)
    o_ref[...] = ((y * g_ref[...]) * (1.0 + sc_ref[...]) + sh_ref[...]).astype(o_ref.dtype)


def _row_tiles(b, t, rows):
    if t >= rows:
        return 1, rows
    bb = max(1, min(b, rows // t))
    while b % bb:
        bb -= 1
    return bb, t


def _norm_mod(x, g, sc, sh):
    b, t, d = x.shape
    bb, tt = _row_tiles(b, t, 512)
    return pl.pallas_call(
        _norm_mod_kernel,
        out_shape=jax.ShapeDtypeStruct((b, t, d), BF16),
        grid=(b // bb, t // tt),
        in_specs=[pl.BlockSpec((bb, tt, d), lambda i, j: (i, j, 0)),
                  pl.BlockSpec((1, d), lambda i, j: (0, 0)),
                  pl.BlockSpec((bb, 1, d), lambda i, j: (i, 0, 0)),
                  pl.BlockSpec((bb, 1, d), lambda i, j: (i, 0, 0))],
        out_specs=pl.BlockSpec((bb, tt, d), lambda i, j: (i, j, 0)),
        compiler_params=_cparams(("arbitrary", "arbitrary"), 32),
        name="norm_mod",
    )(x, g, sc, sh)


def _mm_kernel(a_ref, w_ref, o_ref):
    o_ref[...] = jnp.dot(a_ref[...], w_ref[...], preferred_element_type=F32).astype(o_ref.dtype)


def _matmul(a, w, out_dtype=F32, name="proj"):
    m, k = a.shape
    n = w.shape[1]
    tm = min(m, 1024)
    tn = min(n, 512)
    return pl.pallas_call(
        _mm_kernel,
        out_shape=jax.ShapeDtypeStruct((m, n), out_dtype),
        grid=(m // tm, n // tn),
        in_specs=[pl.BlockSpec((tm, k), lambda i, j: (i, 0)),
                  pl.BlockSpec((k, tn), lambda i, j: (0, j))],
        out_specs=pl.BlockSpec((tm, tn), lambda i, j: (i, j)),
        compiler_params=_cparams(("arbitrary", "arbitrary"), 40),
        name=name,
    )(a, w)


def _merge_out_kernel(oa_ref, ga_ref, ogb_ref, w_ref, x_ref, gt_ref, o_ref, mixed):
    bb, tt, k = oa_ref.shape

    @pl.when(pl.program_id(2) == 0)
    def _():
        mix = _sigmoid(ga_ref[...]) * oa_ref[...] + ogb_ref[...]
        mixed[...] = mix.reshape(bb * tt, k).astype(BF16)

    y = jnp.dot(mixed[...], w_ref[...], preferred_element_type=F32)
    o_ref[...] = x_ref[...] + gt_ref[...] * y.reshape(bb, tt, -1)


def _merge_out(oa, ga, ogb, w, x, gt):
    b, t, k = oa.shape
    n = w.shape[1]
    bb, tt = _row_tiles(b, t, 256)
    tn = 512
    row = lambda i, s, j: (i, s, 0)
    return pl.pallas_call(
        _merge_out_kernel,
        out_shape=jax.ShapeDtypeStruct((b, t, n), F32),
        grid=(b // bb, t // tt, n // tn),
        in_specs=[pl.BlockSpec((bb, tt, k), row),
                  pl.BlockSpec((bb, tt, k), row),
                  pl.BlockSpec((bb, tt, k), row),
                  pl.BlockSpec((k, tn), lambda i, s, j: (0, j)),
                  pl.BlockSpec((bb, tt, tn), lambda i, s, j: (i, s, j)),
                  pl.BlockSpec((bb, 1, tn), lambda i, s, j: (i, 0, j))],
        out_specs=pl.BlockSpec((bb, tt, tn), lambda i, s, j: (i, s, j)),
        scratch_shapes=[pltpu.VMEM((bb * tt, k), BF16)],
        compiler_params=_cparams(("arbitrary", "arbitrary", "arbitrary"), 40),
        name="merge_out",
    )(oa, ga, ogb, w, x, gt)


def _ffn_kernel(h_ref, wg_ref, wu_ref, wd_ref, cw_ref, cb_ref, st_ref, x_ref, gt_ref, gf_ref,
                y_ref, tail_ref, acc, carry):
    s = pl.program_id(1)
    f = pl.program_id(2)
    nf = pl.num_programs(2)
    bb, tt, d = h_ref.shape
    tm = bb * tt
    tf = wg_ref.shape[1]

    h = h_ref[...].reshape(tm, d)
    gate = jnp.dot(h, wg_ref[...], preferred_element_type=F32)
    up = jnp.dot(h, wu_ref[...], preferred_element_type=F32)

    st = st_ref[...]
    first = s == 0

    @pl.when(first)
    def _():
        carry[f] = jnp.zeros((SUBLANES, tf), F32)

    prev = carry[f]
    p0 = jnp.where(first, st[:, 0:1, :], prev[6:7, :][None])
    p1 = jnp.where(first, st[:, 1:2, :], prev[7:8, :][None])
    p0 = jnp.broadcast_to(p0, (bb, tt, tf)).reshape(tm, tf)
    p1 = jnp.broadcast_to(p1, (bb, tt, tf)).reshape(tm, tf)
    tpos = lax.broadcasted_iota(I32, (tm, tf), 0) % tt
    r1 = pltpu.roll(gate, 1, axis=0)
    r2 = pltpu.roll(gate, 2, axis=0)
    x1 = jnp.where(tpos == 0, p1, r1)
    x2 = jnp.where(tpos == 0, p0, jnp.where(tpos == 1, p1, r2))
    cw = cw_ref[...]
    a = cw[0:1] * x2 + cw[1:2] * x1 + cw[2:3] * gate + cb_ref[...]
    act = (_silu(a) * up).astype(BF16)
    contrib = jnp.dot(act, wd_ref[...], preferred_element_type=F32)

    g3 = gate.reshape(bb, tt, tf)
    tail_ref[...] = g3[:, None, tt - SUBLANES:, :]
    carry[f] = gate[tm - SUBLANES:, :]

    @pl.when(f == 0)
    def _():
        acc[...] = contrib

    @pl.when(f > 0)
    def _():
        acc[...] += contrib

    @pl.when(f == nf - 1)
    def _():
        x = x_ref[...] + gt_ref[...] * acc[...].reshape(bb, tt, d)
        y = x * lax.rsqrt(jnp.mean(x * x, axis=-1, keepdims=True) + EPS)
        y_ref[...] = y * gf_ref[...]


def _ffn(h2, wg, wu, wd, cw, cb, st, x1, gt2, gf):
    b, t, d = h2.shape
    dff = wg.shape[1]
    bb, tt = _row_tiles(b, t, 512)
    tf = 512
    nf = dff // tf
    row = lambda i, s, f: (i, s, 0)
    return pl.pallas_call(
        _ffn_kernel,
        out_shape=(jax.ShapeDtypeStruct((b, t, d), F32),
                   jax.ShapeDtypeStruct((b, t // tt, SUBLANES, dff), F32)),
        grid=(b // bb, t // tt, nf),
        in_specs=[pl.BlockSpec((bb, tt, d), row),
                  pl.BlockSpec((d, tf), lambda i, s, f: (0, f)),
                  pl.BlockSpec((d, tf), lambda i, s, f: (0, f)),
                  pl.BlockSpec((tf, d), lambda i, s, f: (f, 0)),
                  pl.BlockSpec((FFN_CONV_WIDTH, tf), lambda i, s, f: (0, f)),
                  pl.BlockSpec((1, tf), lambda i, s, f: (0, f)),
                  pl.BlockSpec((bb, FFN_CONV_WIDTH - 1, tf), lambda i, s, f: (i, 0, f)),
                  pl.BlockSpec((bb, tt, d), row),
                  pl.BlockSpec((bb, 1, d), lambda i, s, f: (i, 0, 0)),
                  pl.BlockSpec((1, d), lambda i, s, f: (0, 0))],
        out_specs=(pl.BlockSpec((bb, tt, d), row),
                   pl.BlockSpec((bb, 1, SUBLANES, tf), lambda i, s, f: (i, s, 0, f))),
        scratch_shapes=[pltpu.VMEM((bb * tt, d), F32),
                        pltpu.VMEM((nf, SUBLANES, tf), F32)],
        compiler_params=_cparams(("arbitrary", "arbitrary", "arbitrary"), 48),
        name="conv_ffn",
    )(h2, wg, wu, wd, cw, cb, st, x1, gt2, gf)


HEADS_PER_GROUP = 4
GROUP_ROWS = HEADS_PER_GROUP * CHUNK


def _delta_kernel(x_ref, st_ref, cw_ref, ab_ref, hp_ref, s0_ref, z_ref, gb_ref, gdn_ref,
                  o_ref, s_ref, carry, *, nheads):
    c = pl.program_id(1)
    r = GROUP_ROWS
    hd = DK_B

    @pl.when(c == 0)
    def _():
        carry[...] = st_ref[0]
        s_ref[...] = s0_ref[...]

    prev = carry[...]
    carry[...] = x_ref[0, CHUNK - SUBLANES:, :]

    ri = lax.broadcasted_iota(I32, (r, r), 0)
    ci = lax.broadcasted_iota(I32, (r, r), 1)
    same_head = (ri // CHUNK) == (ci // CHUNK)
    incl = same_head & (ri >= ci)
    strict = same_head & (ri > ci)
    triu = jnp.where(same_head & (ri <= ci), 1.0, 0.0).astype(F32)
    eye = jnp.where(ri == ci, 1.0, 0.0).astype(F32)
    row8 = lax.broadcasted_iota(I32, (SUBLANES, r * 2), 0)
    rid = lax.broadcasted_iota(I32, (LANES, r), 0)

    def conv_silu(col0, width):
        xs = x_ref[0, :, col0:col0 + width]
        pv = prev[:, col0:col0 + width]
        w = cw_ref[:, col0:col0 + width]
        y = xs * w[CONVB_WIDTH - 1:CONVB_WIDTH]
        for k in range(1, CONVB_WIDTH):
            rolled = pltpu.roll(xs, k, axis=0)
            head = jnp.where(row8[:, :width] < k, pltpu.roll(pv, k, axis=0), rolled[:SUBLANES])
            shifted = jnp.concatenate([head, rolled[SUBLANES:]], axis=0)
            y = y + shifted * w[CONVB_WIDTH - 1 - k:CONVB_WIDTH - k]
        return _silu(y)

    def l2n(v):
        return v * lax.rsqrt(jnp.sum(v * v, axis=-1, keepdims=True) + EPS)

    def stack(y, fn):
        return jnp.concatenate([fn(y[:, j * hd:(j + 1) * hd]) for j in range(HEADS_PER_GROUP)], axis=0)

    for g in range(nheads // HEADS_PER_GROUP):
        gw = HEADS_PER_GROUP * hd
        qs = stack(conv_silu(g * gw, gw), lambda v: l2n(v) * (DK_B ** -0.5))
        ks = stack(conv_silu(nheads * hd + g * gw, gw), l2n)
        vs = stack(conv_silu(2 * nheads * hd + g * gw, gw), lambda v: v)

        a_row = ab_ref[0, 0, 0:1, g * r:(g + 1) * r]
        b_row = ab_ref[0, 0, 1:2, g * r:(g + 1) * r]
        alog = hp_ref[0:1, g * r:(g + 1) * r]
        dtb = hp_ref[1:2, g * r:(g + 1) * r]
        xsp = a_row + dtb
        softplus = jnp.maximum(xsp, 0.0) + jnp.log1p(jnp.exp(-jnp.abs(xsp)))
        g_row = -jnp.exp(alog) * softplus
        beta_row = _sigmoid(b_row)
        gc_row = _dot(jnp.broadcast_to(g_row, (SUBLANES, r)), triu, passes=6)[0:1]
        rows = jnp.where(rid == 0, beta_row, jnp.where(rid == 1, gc_row, 0.0))
        cols = rows.T
        beta_col = cols[:, 0:1]
        gc_col = cols[:, 1:2]
        gl_col = jnp.concatenate(
            [jnp.broadcast_to(cols[(j + 1) * CHUNK - 1:(j + 1) * CHUNK, 1:2], (CHUNK, 1))
             for j in range(HEADS_PER_GROUP)], axis=0)

        decay = jnp.where(incl, jnp.exp(gc_col - gc_row), 0.0)
        egc = jnp.exp(gc_col)
        kb = ks * beta_col
        vb = vs * beta_col
        m = jnp.where(strict, _dot(kb, ks, _NT, passes=3) * decay, 0.0)

        inv = eye - jnp.where((ri // 2 == ci // 2), m, 0.0)
        bs = 2
        while bs < CHUNK:
            lower = ((ri // (2 * bs)) == (ci // (2 * bs))) & ((ri % (2 * bs)) >= bs) & ((ci % (2 * bs)) < bs)
            cblk = jnp.where(lower, m, 0.0)
            inv = inv - _dot(_dot(inv, cblk, passes=3), inv, passes=3)
            bs *= 2

        uw = _dot(inv, jnp.concatenate([vb, kb * egc], axis=1), passes=3)
        u = uw[:, :hd]
        w = uw[:, hd:]
        attn = jnp.where(incl, _dot(qs, ks, _NT, passes=3) * decay, 0.0)
        qg = qs * egc
        kd = ks * jnp.exp(gl_col - gc_col)

        vnew = []
        qstate = []
        for j in range(HEADS_PER_GROUP):
            hidx = g * HEADS_PER_GROUP + j
            sl = slice(j * CHUNK, (j + 1) * CHUNK)
            wq = jnp.concatenate([w[sl], qg[sl]], axis=0)
            rs = _dot(wq, s_ref[0, hidx], passes=3)
            vnew.append(u[sl] - rs[:CHUNK])
            qstate.append(rs[CHUNK:])
        vnew_s = jnp.concatenate(vnew, axis=0)
        intra = _dot(attn, vnew_s, passes=3)

        for j in range(HEADS_PER_GROUP):
            hidx = g * HEADS_PER_GROUP + j
            sl = slice(j * CHUNK, (j + 1) * CHUNK)
            hs = slice(hidx * hd, (hidx + 1) * hd)
            o = qstate[j] + intra[sl]
            glast = jnp.exp(gl_col[j * CHUNK:j * CHUNK + 1, :])
            s_ref[0, hidx] = s_ref[0, hidx] * glast + _dot(kd[sl], vnew[j], _TN, passes=3)
            on = o * lax.rsqrt(jnp.mean(o * o, axis=-1, keepdims=True) + EPS) * gdn_ref[...]
            o_ref[0, :, hs] = on * _silu(z_ref[0, :, hs]) * _sigmoid(gb_ref[0, :, hs])


def _delta(xqkv, st8, cw, ab, hp, s0, z, gb, gdn):
    b, t, cdim = xqkv.shape
    nheads = s0.shape[1]
    dm = z.shape[2]
    n = t // CHUNK
    kern = functools.partial(_delta_kernel, nheads=nheads)
    return pl.pallas_call(
        kern,
        out_shape=(jax.ShapeDtypeStruct((b, t, dm), F32),
                   jax.ShapeDtypeStruct(s0.shape, F32)),
        grid=(b, n),
        in_specs=[pl.BlockSpec((1, CHUNK, cdim), lambda i, c: (i, c, 0)),
                  pl.BlockSpec((1, SUBLANES, cdim), lambda i, c: (i, 0, 0)),
                  pl.BlockSpec((CONVB_WIDTH, cdim), lambda i, c: (0, 0)),
                  pl.BlockSpec((1, 1, SUBLANES, nheads * CHUNK), lambda i, c: (i, c, 0, 0)),
                  pl.BlockSpec((SUBLANES, nheads * CHUNK), lambda i, c: (0, 0)),
                  pl.BlockSpec((1, nheads, DK_B, DV_B), lambda i, c: (i, 0, 0, 0)),
                  pl.BlockSpec((1, CHUNK, dm), lambda i, c: (i, c, 0)),
                  pl.BlockSpec((1, CHUNK, dm), lambda i, c: (i, c, 0)),
                  pl.BlockSpec((1, DV_B), lambda i, c: (0, 0))],
        out_specs=(pl.BlockSpec((1, CHUNK, dm), lambda i, c: (i, c, 0)),
                   pl.BlockSpec((1, nheads, DK_B, DV_B), lambda i, c: (i, 0, 0, 0))),
        scratch_shapes=[pltpu.VMEM((SUBLANES, cdim), F32)],
        compiler_params=_cparams(("arbitrary", "arbitrary"), 40),
        name="gated_delta",
    )(xqkv, st8, cw, ab, hp, s0, z, gb, gdn)


_T5_THRESHOLDS = tuple(math.ceil(8 * 2 ** (k / 2)) for k in range(1, 8))
_FAR_BUCKET = NUM_BUCKETS // 2 - 1


def _attn_kernel(rb_ref, q_ref, qi_ref, sm_ref, kit_ref, k_ref, v_ref, o_ref,
                 skey, thr, jcut, qsc, qis, wib, m_scr, l_scr, acc, dtile,
                 *, past, qb, tk, n_sel, nheads):
    b = pl.program_id(0)
    i = pl.program_id(1)
    kt = pl.program_id(2)
    hd = HD_A
    rep = tk // LANES
    kd = past // tk + i * (qb // tk)
    q0 = past + i * qb

    def lanes(v):
        return jnp.concatenate([v] * rep, axis=1) if rep > 1 else v

    @pl.when((b == 0) & (i == 0) & (kt == 0))
    def _():
        rr = lax.broadcasted_iota(I32, (qb, tk), 0)
        cc = lax.broadcasted_iota(I32, (qb, tk), 1)
        for which in range(2):
            rel = cc - rr - which * tk
            n = jnp.abs(rel)
            large = jnp.full((qb, tk), NUM_BUCKETS // 4, I32)
            for th in _T5_THRESHOLDS:
                large = large + jnp.where(n >= th, 1, 0)
            bucket = jnp.where(rel > 0, NUM_BUCKETS // 2, 0) + jnp.where(n < NUM_BUCKETS // 4, n, large)

            def head_body(h, _, bucket=bucket, which=which):
                base = rb_ref[_FAR_BUCKET, h]
                tile = jnp.zeros((qb, tk), F32)
                for bk in range(NUM_BUCKETS):
                    if bk != _FAR_BUCKET:
                        tile = jnp.where(bucket == bk, (rb_ref[bk, h] - base) * LOG2E, tile)
                dtile[which, h] = tile
                return 0

            lax.fori_loop(0, nheads, head_body, 0)

    @pl.when(kt == 0)
    def _():
        qsc[...] = (q_ref[0] * (HD_A ** -0.5 * LOG2E)).astype(BF16)
        sm = sm_ref[0]
        for h in range(H_I):
            qis[h] = qi_ref[0, :, h * D_IDX:(h + 1) * D_IDX].astype(BF16)
            wcol = sm[:, D_IDX + h:D_IDX + h + 1] * ((H_I * D_IDX) ** -0.5)
            wib[h] = jnp.broadcast_to(wcol, (qb, LANES))
        m_scr[...] = jnp.full(m_scr.shape, -jnp.inf, F32)
        l_scr[...] = jnp.zeros(l_scr.shape, F32)
        acc[...] = jnp.zeros(acc.shape, F32)

        qpos = q0 + lax.broadcasted_iota(I32, (qb, tk), 0)
        lim = (qpos // CHUNK + 1) * CHUNK
        lane = lax.broadcasted_iota(I32, (qb, tk), 1)

        def score_tile(j, _):
            off = pl.multiple_of(j * tk, tk)
            kit = kit_ref[0, :, pl.ds(off, tk)]
            sc = jnp.zeros((qb, tk), F32)
            for h in range(H_I):
                d = jnp.dot(qis[h], kit, preferred_element_type=F32)
                sc = sc + jnp.maximum(d, 0.0) * lanes(wib[h])
            bits = pltpu.bitcast(sc + 0.0, I32)
            key = bits ^ ((bits >> 31) & 0x7FFFFFFF)
            key = jnp.where(lane + off < lim, key, INT_MIN)
            skey[:, pl.ds(off, tk)] = key
            return 0

        lax.fori_loop(0, kd + 1, score_tile, 0)

        def count(pred):
            def body(j, c):
                off = pl.multiple_of(j * tk, tk)
                mk = jnp.where(pred(skey[:, pl.ds(off, tk)], off), 1, 0)
                part = mk[:, :LANES]
                for u in range(1, rep):
                    part = part + mk[:, u * LANES:(u + 1) * LANES]
                return c + part
            c = lax.fori_loop(0, kd + 1, body, jnp.zeros((qb, LANES), I32))
            return jnp.broadcast_to(jnp.sum(c, axis=1, keepdims=True), (qb, LANES))

        def bit_body(bi, tu):
            cand_u = tu | lax.shift_left(jnp.int32(1), jnp.int32(31) - bi)
            cand_s = lanes(cand_u ^ INT_MIN)
            cnt = count(lambda x, off: x >= cand_s)
            return jnp.where(cnt >= n_sel, cand_u, tu)

        tu = lax.fori_loop(0, 32, bit_body, jnp.zeros((qb, LANES), I32))
        ts = jnp.maximum(tu ^ INT_MIN, INT_MIN + 1)
        thr[...] = ts
        jcut[...] = jnp.full((qb, LANES), 2 ** 30, I32)

        tsl = lanes(ts)
        c_gt = count(lambda x, off: x > tsl)
        c_eq = count(lambda x, off: x == tsl)
        need = n_sel - c_gt
        excess = jnp.max(jnp.where((c_eq > need) & (tu != 0), 1.0, 0.0))

        @pl.when(excess > 0.5)
        def _():
            def jbit(bi, jc):
                cand = jc | lax.shift_left(jnp.int32(1), jnp.int32(15) - bi)
                cl = lanes(cand)
                cnt = count(lambda x, off: (x == tsl) & (lane + off < cl))
                return jnp.where(cnt <= need, cand, jc)
            jcut[...] = lax.fori_loop(0, 16, jbit, jnp.zeros((qb, LANES), I32))

    def flash(bias_of_head):
        off = pl.multiple_of(kt * tk, tk)
        x = skey[:, pl.ds(off, tk)]
        ts = lanes(thr[...])
        lane = lax.broadcasted_iota(I32, (qb, tk), 1) + off
        sel = (x > ts) | ((x == ts) & (lane < lanes(jcut[...])))
        for h in range(nheads):
            hs = slice(h * hd, (h + 1) * hd)
            s = lax.dot_general(qsc[:, hs], k_ref[0, :, hs], _NT, preferred_element_type=F32)
            s = bias_of_head(s, h)
            s = jnp.where(sel, s, -jnp.inf)
            m_prev = m_scr[h]
            m_new = jnp.maximum(m_prev, jnp.max(s, axis=1, keepdims=True))
            m_safe = jnp.where(m_new == -jnp.inf, 0.0, m_new)
            alpha = jnp.exp2(m_prev - m_safe)
            p = jnp.exp2(s - lanes(m_safe))
            l_scr[h] = alpha * l_scr[h] + jnp.sum(p, axis=1, keepdims=True)
            m_scr[h] = m_new
            pv = jnp.dot(p.astype(BF16), v_ref[0, :, hs], preferred_element_type=F32)
            acc[:, hs] = alpha * acc[:, hs] + pv

    @pl.when(kt < kd - 1)
    def _():
        flash(lambda s, h: s)

    @pl.when((kt >= kd - 1) & (kt <= kd))
    def _():
        which = kd - kt
        flash(lambda s, h: s + dtile[which, h])

    @pl.when(kt == kd)
    def _():
        for h in range(nheads):
            hs = slice(h * hd, (h + 1) * hd)
            o_ref[0, :, hs] = (acc[:, hs] / l_scr[h]).astype(o_ref.dtype)


def _attention(rel_bias, q, qi, small, kit, kall, vall, *, past, n_sel):
    b, t, dm = q.shape
    lp = kall.shape[1]
    nheads = dm // HD_A
    qb = min(t, 256)
    tk = 256
    nq = t // qb
    nkt = lp // tk
    assert past % tk == 0 and lp % tk == 0 and t % qb == 0
    assert qb == tk or nq == 1
    assert lp < 2 ** 16
    kdiag = lambda i: past // tk + i * (qb // tk)
    kern = functools.partial(_attn_kernel, past=past, qb=qb, tk=tk, n_sel=n_sel, nheads=nheads)
    row = lambda bi, i, kt: (bi, i, 0)
    kv = lambda bi, i, kt: (bi, jnp.minimum(kt, kdiag(i)), 0)
    return pl.pallas_call(
        kern,
        out_shape=jax.ShapeDtypeStruct((b, t, dm), F32),
        grid=(b, nq, nkt),
        in_specs=[pl.BlockSpec(memory_space=pltpu.SMEM),
                  pl.BlockSpec((1, qb, dm), row),
                  pl.BlockSpec((1, qb, H_I * D_IDX), row),
                  pl.BlockSpec((1, qb, LANES), row),
                  pl.BlockSpec((1, D_IDX, lp), lambda bi, i, kt: (bi, 0, 0)),
                  pl.BlockSpec((1, tk, dm), kv),
                  pl.BlockSpec((1, tk, dm), kv)],
        out_specs=pl.BlockSpec((1, qb, dm), row),
        scratch_shapes=[pltpu.VMEM((qb, lp), I32),
                        pltpu.VMEM((qb, LANES), I32),
                        pltpu.VMEM((qb, LANES), I32),
                        pltpu.VMEM((qb, dm), BF16),
                        pltpu.VMEM((H_I, qb, D_IDX), BF16),
                        pltpu.VMEM((H_I, qb, LANES), F32),
                        pltpu.VMEM((nheads, qb, LANES), F32),
                        pltpu.VMEM((nheads, qb, LANES), F32),
                        pltpu.VMEM((qb, dm), F32),
                        pltpu.VMEM((2, nheads, qb, tk), F32)],
        compiler_params=_cparams(("arbitrary", "arbitrary", "arbitrary"), 60),
        name="sparse_attn",
    )(rel_bias, q, qi, small, kit, kall, vall)


def _pad_rows(a, rows):
    return jnp.pad(a, [(0, 0), (rows - a.shape[1], 0)] + [(0, 0)] * (a.ndim - 2))


def _run_group(x, mod, cache_k, cache_v, cache_kidx, st_conv, st_delta, st_ffn, rel_bias, g_final, w):
    b, t, d = x.shape
    nheads = d // HD_A
    sh1, sc1, gt1, sh2, sc2, gt2 = [m[:, None, :] for m in jnp.split(mod, 6, axis=-1)]
    m_rows = b * t

    h = _norm_mod(x, w["g_norm1"], sc1, sh1).reshape(m_rows, d)
    proj = lambda name: _matmul(h, w[name], name="proj_" + name).reshape(b, t, -1)
    qa, ka, va, qi = proj("w_qa"), proj("w_ka"), proj("w_va"), proj("w_qi")
    small = proj("w_small")
    qkv_b, z_b, gate_a, gate_b = proj("w_qkvb"), proj("w_z"), proj("w_ga"), proj("w_gb")
    ki = small[..., :D_IDX]

    past = 0 if cache_k is None else cache_k.shape[1]
    l_all = past + t
    tk = 256
    lp = -(-l_all // tk) * tk
    if cache_k is None:
        k_all, v_all, ki_all = ka, va, ki
    else:
        k_all = jnp.concatenate([cache_k.reshape(b, past, d), ka], axis=1)
        v_all = jnp.concatenate([cache_v.reshape(b, past, d), va], axis=1)
        ki_all = jnp.concatenate([cache_kidx, ki], axis=1)
    padk = lambda a: jnp.pad(a, ((0, 0), (0, lp - l_all), (0, 0))).astype(BF16)
    k_all, v_all = padk(k_all), padk(v_all)
    kit = jnp.swapaxes(padk(ki_all), 1, 2)
    n_sel = min(TOPK_MAX, l_all // 4)
    o_a = _attention(rel_bias, qa, qi, small, kit, k_all, v_all, past=past, n_sel=n_sel)

    n = t // CHUNK
    to_rows = lambda v: jnp.swapaxes(v.reshape(b, n, CHUNK, nheads), 2, 3).reshape(b, n, 1, nheads * CHUNK)
    ab = jnp.concatenate([to_rows(small[..., 80:96]), to_rows(small[..., 96:112]),
                          jnp.zeros((b, n, SUBLANES - 2, nheads * CHUNK), F32)], axis=2)
    ogb, new_delta = _delta(qkv_b, _pad_rows(st_conv, SUBLANES), w["conv_b_w"], ab, w["hp_rows"],
                            st_delta, z_b, gate_b, w["g_delta_norm"])
    assert t >= CONVB_WIDTH - 1
    new_conv = qkv_b[:, t - (CONVB_WIDTH - 1):]

    x1 = _merge_out(o_a, gate_a, ogb, w["w_out"], x, gt1)
    h2 = _norm_mod(x1, w["g_norm2"], sc2, sh2)
    y, tail = _ffn(h2, w["w_ffn_gate"], w["w_ffn_up"], w["w_ffn_down"], w["ffn_conv_w"], w["ffn_conv_b"],
                   st_ffn, x1, gt2, g_final)
    new_ffn = tail[:, -1, SUBLANES - (FFN_CONV_WIDTH - 1):]
    kshape = (1, b, t, nheads, HD_A)
    return (y, ka.reshape(kshape), va.reshape(kshape), ki[None], new_conv[None], new_delta[None], new_ffn[None])


def kernel(x_prompt, x_sample, c_prompt, c_sample, cache_k, cache_v, cache_kidx, state_conv_b, state_delta,
           state_ffn_conv, rel_bias, g_final, w_ada, b_ada, g_norm1, w_in, conv_b_w, a_log, dt_bias,
           g_delta_norm, w_out, g_norm2, w_ffn_gate, ffn_conv_w, ffn_conv_b, w_ffn_up, w_ffn_down):
    assert w_ada.shape[0] == 1, "one layer"
    d = x_prompt.shape[-1]
    nheads = d // HD_A
    bp, bs = x_prompt.shape[0], x_sample.shape[0]
    convb = 3 * nheads * DK_B

    wi_ = w_in[0]
    sizes = (d, d, d, H_I * D_IDX, D_IDX, H_I, convb, nheads, nheads, d, d, d)
    offs = [0]
    for s in sizes:
        offs.append(offs[-1] + s)
    col = lambda k: wi_[:, offs[k]:offs[k + 1]]
    w_small = jnp.concatenate([col(4), col(5), col(7), col(8),
                               jnp.zeros((d, LANES - D_IDX - H_I - 2 * nheads), F32)], axis=1)
    bf = lambda a: a.astype(BF16)
    w = dict(
        w_qa=bf(col(0)), w_ka=bf(col(1)), w_va=bf(col(2)), w_qi=bf(col(3)), w_small=bf(w_small),
        w_qkvb=bf(col(6)), w_z=bf(col(9)), w_ga=bf(col(10)), w_gb=bf(col(11)),
        g_norm1=g_norm1, g_norm2=g_norm2, conv_b_w=conv_b_w[0], g_delta_norm=g_delta_norm,
        w_out=bf(w_out[0]), w_ffn_gate=bf(w_ffn_gate[0]), w_ffn_up=bf(w_ffn_up[0]),
        w_ffn_down=bf(w_ffn_down[0]), ffn_conv_w=ffn_conv_w[0], ffn_conv_b=ffn_conv_b,
        hp_rows=jnp.concatenate([jnp.repeat(a_log[0], CHUNK)[None], jnp.repeat(dt_bias[0], CHUNK)[None],
                                 jnp.zeros((SUBLANES - 2, nheads * CHUNK), F32)], axis=0),
    )

    c_all = jnp.concatenate([c_prompt, c_sample], axis=0)
    rows = -(-c_all.shape[0] // SUBLANES) * SUBLANES
    c_all = jnp.pad(c_all, ((0, rows - c_all.shape[0]), (0, 0)))
    mod = _adaln(c_all, w_ada[0], b_ada)

    zeros = lambda *s: jnp.zeros(s, F32)
    outs_p = _run_group(x_prompt, mod[:bp], None, None, None,
                        zeros(bp, CONVB_WIDTH - 1, convb), zeros(bp, nheads, DK_B, DV_B),
                        zeros(bp, FFN_CONV_WIDTH - 1, w_ffn_gate.shape[-1]), rel_bias, g_final[None], w)
    outs_s = _run_group(x_sample, mod[bp:bp + bs], cache_k[0], cache_v[0], cache_kidx[0],
                        state_conv_b[0], state_delta[0], state_ffn_conv[0], rel_bias, g_final[None], w)
    return (outs_p[0], outs_s[0]) + tuple(outs_p[1:]) + tuple(outs_s[1:])
```

```python
import functools
import math

import jax
import jax.numpy as jnp
from jax import lax
from jax.experimental import pallas as pl
from jax.experimental.pallas import tpu as pltpu

F32 = jnp.float32
BF16 = jnp.bfloat16
I32 = jnp.int32

CHUNK = 64
HD_A = 128
H_I = 16
D_IDX = 64
TOPK_MAX = 256
NUM_BUCKETS = 32
MAX_DISTANCE = 128
DK_B = 128
DV_B = 128
CONVB_WIDTH = 4
FFN_CONV_WIDTH = 3
EPS = 1e-6
LOG2E = math.log2(math.e)

V7X_VMEM_BYTES = 64 * 1024 * 1024
LANES = 128
SUBLANES = 8
INT_MIN = -(2 ** 31)


def _cparams(sem, vmem_mb):
    assert vmem_mb * 1024 * 1024 < V7X_VMEM_BYTES
    return pltpu.CompilerParams(dimension_semantics=sem, vmem_limit_bytes=vmem_mb * 1024 * 1024)


def _sigmoid(x):
    return 1.0 / (1.0 + jnp.exp(-x))


def _silu(x):
    return x * _sigmoid(x)


def _dot(a, b, dims=None, passes=1):
    dn = dims if dims is not None else (((a.ndim - 1,), (0,)), ((), ()))
    if passes == 6:
        return lax.dot_general(a, b, dn, precision=lax.Precision.HIGHEST, preferred_element_type=F32)
    ah = a.astype(BF16)
    bh = b.astype(BF16)
    out = lax.dot_general(ah, bh, dn, preferred_element_type=F32)
    if passes == 3:
        al = (a - ah.astype(F32)).astype(BF16)
        bl = (b - bh.astype(F32)).astype(BF16)
        out = out + lax.dot_general(ah, bl, dn, preferred_element_type=F32)
        out = out + lax.dot_general(al, bh, dn, preferred_element_type=F32)
    return out


_NT = (((1,), (1,)), ((), ()))
_TN = (((0,), (0,)), ((), ()))


def _adaln_kernel(c_ref, w_ref, b_ref, o_ref):
    o_ref[...] = _dot(_silu(c_ref[...]), w_ref[...]) + b_ref[...]


def _adaln(c, w, b):
    bp, d = c.shape
    n = w.shape[1]
    tn = 1024
    return pl.pallas_call(
        _adaln_kernel,
        out_shape=jax.ShapeDtypeStruct((bp, n), F32),
        grid=(n // tn,),
        in_specs=[pl.BlockSpec((bp, d), lambda j: (0, 0)),
                  pl.BlockSpec((d, tn), lambda j: (0, j)),
                  pl.BlockSpec((1, tn), lambda j: (0, j))],
        out_specs=pl.BlockSpec((bp, tn), lambda j: (0, j)),
        compiler_params=_cparams(("arbitrary",), 40),
        name="adaln",
    )(c, w, b)


def _norm_mod_kernel(x_ref, g_ref, sc_ref, sh_ref, o_ref):
    x = x_ref[...]
    y = x * lax.rsqrt(jnp.mean(x * x, axis=-1, keepdims=True) + EPS)
    o_ref[...] = ((y * g_ref[...]) * (1.0 + sc_ref[...]) + sh_ref[...]).astype(o_ref.dtype)


def _row_tiles(b, t, rows):
    if t >= rows:
        return 1, rows
    bb = max(1, min(b, rows // t))
    while b % bb:
        bb -= 1
    return bb, t


def _norm_mod(x, g, sc, sh):
    b, t, d = x.shape
    bb, tt = _row_tiles(b, t, 512)
    return pl.pallas_call(
        _norm_mod_kernel,
        out_shape=jax.ShapeDtypeStruct((b, t, d), BF16),
        grid=(b // bb, t // tt),
        in_specs=[pl.BlockSpec((bb, tt, d), lambda i, j: (i, j, 0)),
                  pl.BlockSpec((1, d), lambda i, j: (0, 0)),
                  pl.BlockSpec((bb, 1, d), lambda i, j: (i, 0, 0)),
                  pl.BlockSpec((bb, 1, d), lambda i, j: (i, 0, 0))],
        out_specs=pl.BlockSpec((bb, tt, d), lambda i, j: (i, j, 0)),
        compiler_params=_cparams(("arbitrary", "arbitrary"), 32),
        name="norm_mod",
    )(x, g, sc, sh)


def _mm_kernel(a_ref, w_ref, o_ref):
    o_ref[...] = jnp.dot(a_ref[...], w_ref[...], preferred_element_type=F32).astype(o_ref.dtype)


def _matmul(a, w, out_dtype=F32, name="proj"):
    m, k = a.shape
    n = w.shape[1]
    tm = min(m, 1024)
    tn = min(n, 512)
    return pl.pallas_call(
        _mm_kernel,
        out_shape=jax.ShapeDtypeStruct((m, n), out_dtype),
        grid=(m // tm, n // tn),
        in_specs=[pl.BlockSpec((tm, k), lambda i, j: (i, 0)),
                  pl.BlockSpec((k, tn), lambda i, j: (0, j))],
        out_specs=pl.BlockSpec((tm, tn), lambda i, j: (i, j)),
        compiler_params=_cparams(("arbitrary", "arbitrary"), 40),
        name=name,
    )(a, w)


def _merge_out_kernel(oa_ref, ga_ref, ogb_ref, w_ref, x_ref, gt_ref, o_ref, mixed):
    bb, tt, k = oa_ref.shape

    @pl.when(pl.program_id(2) == 0)
    def _():
        mix = _sigmoid(ga_ref[...]) * oa_ref[...] + ogb_ref[...]
        mixed[...] = mix.reshape(bb * tt, k).astype(BF16)

    y = jnp.dot(mixed[...], w_ref[...], preferred_element_type=F32)
    o_ref[...] = x_ref[...] + gt_ref[...] * y.reshape(bb, tt, -1)


def _merge_out(oa, ga, ogb, w, x, gt):
    b, t, k = oa.shape
    n = w.shape[1]
    bb, tt = _row_tiles(b, t, 256)
    tn = 512
    row = lambda i, s, j: (i, s, 0)
    return pl.pallas_call(
        _merge_out_kernel,
        out_shape=jax.ShapeDtypeStruct((b, t, n), F32),
        grid=(b // bb, t // tt, n // tn),
        in_specs=[pl.BlockSpec((bb, tt, k), row),
                  pl.BlockSpec((bb, tt, k), row),
                  pl.BlockSpec((bb, tt, k), row),
                  pl.BlockSpec((k, tn), lambda i, s, j: (0, j)),
                  pl.BlockSpec((bb, tt, tn), lambda i, s, j: (i, s, j)),
                  pl.BlockSpec((bb, 1, tn), lambda i, s, j: (i, 0, j))],
        out_specs=pl.BlockSpec((bb, tt, tn), lambda i, s, j: (i, s, j)),
        scratch_shapes=[pltpu.VMEM((bb * tt, k), BF16)],
        compiler_params=_cparams(("arbitrary", "arbitrary", "arbitrary"), 40),
        name="merge_out",
    )(oa, ga, ogb, w, x, gt)


def _ffn_kernel(h_ref, wg_ref, wu_ref, wd_ref, cw_ref, cb_ref, st_ref, x_ref, gt_ref, gf_ref,
                y_ref, tail_ref, acc, carry):
    s = pl.program_id(1)
    f = pl.program_id(2)
    nf = pl.num_programs(2)
    bb, tt, d = h_ref.shape
    tm = bb * tt
    tf = wg_ref.shape[1]

    h = h_ref[...].reshape(tm, d)
    gate = jnp.dot(h, wg_ref[...], preferred_element_type=F32)
    up = jnp.dot(h, wu_ref[...], preferred_element_type=F32)

    st = st_ref[...]
    first = s == 0

    @pl.when(first)
    def _():
        carry[f] = jnp.zeros((SUBLANES, tf), F32)

    prev = carry[f]
    p0 = jnp.where(first, st[:, 0:1, :], prev[6:7, :][None])
    p1 = jnp.where(first, st[:, 1:2, :], prev[7:8, :][None])
    p0 = jnp.broadcast_to(p0, (bb, tt, tf)).reshape(tm, tf)
    p1 = jnp.broadcast_to(p1, (bb, tt, tf)).reshape(tm, tf)
    tpos = lax.broadcasted_iota(I32, (tm, tf), 0) % tt
    r1 = pltpu.roll(gate, 1, axis=0)
    r2 = pltpu.roll(gate, 2, axis=0)
    x1 = jnp.where(tpos == 0, p1, r1)
    x2 = jnp.where(tpos == 0, p0, jnp.where(tpos == 1, p1, r2))
    cw = cw_ref[...]
    a = cw[0:1] * x2 + cw[1:2] * x1 + cw[2:3] * gate + cb_ref[...]
    act = (_silu(a) * up).astype(BF16)
    contrib = jnp.dot(act, wd_ref[...], preferred_element_type=F32)

    g3 = gate.reshape(bb, tt, tf)
    tail_ref[...] = g3[:, None, tt - SUBLANES:, :]
    carry[f] = gate[tm - SUBLANES:, :]

    @pl.when(f == 0)
    def _():
        acc[...] = contrib

    @pl.when(f > 0)
    def _():
        acc[...] += contrib

    @pl.when(f == nf - 1)
    def _():
        x = x_ref[...] + gt_ref[...] * acc[...].reshape(bb, tt, d)
        y = x * lax.rsqrt(jnp.mean(x * x, axis=-1, keepdims=True) + EPS)
        y_ref[...] = y * gf_ref[...]


def _ffn(h2, wg, wu, wd, cw, cb, st, x1, gt2, gf):
    b, t, d = h2.shape
    dff = wg.shape[1]
    bb, tt = _row_tiles(b, t, 512)
    tf = 512
    nf = dff // tf
    row = lambda i, s, f: (i, s, 0)
    return pl.pallas_call(
        _ffn_kernel,
        out_shape=(jax.ShapeDtypeStruct((b, t, d), F32),
                   jax.ShapeDtypeStruct((b, t // tt, SUBLANES, dff), F32)),
        grid=(b // bb, t // tt, nf),
        in_specs=[pl.BlockSpec((bb, tt, d), row),
                  pl.BlockSpec((d, tf), lambda i, s, f: (0, f)),
                  pl.BlockSpec((d, tf), lambda i, s, f: (0, f)),
                  pl.BlockSpec((tf, d), lambda i, s, f: (f, 0)),
                  pl.BlockSpec((FFN_CONV_WIDTH, tf), lambda i, s, f: (0, f)),
                  pl.BlockSpec((1, tf), lambda i, s, f: (0, f)),
                  pl.BlockSpec((bb, FFN_CONV_WIDTH - 1, tf), lambda i, s, f: (i, 0, f)),
                  pl.BlockSpec((bb, tt, d), row),
                  pl.BlockSpec((bb, 1, d), lambda i, s, f: (i, 0, 0)),
                  pl.BlockSpec((1, d), lambda i, s, f: (0, 0))],
        out_specs=(pl.BlockSpec((bb, tt, d), row),
                   pl.BlockSpec((bb, 1, SUBLANES, tf), lambda i, s, f: (i, s, 0, f))),
        scratch_shapes=[pltpu.VMEM((bb * tt, d), F32),
                        pltpu.VMEM((nf, SUBLANES, tf), F32)],
        compiler_params=_cparams(("arbitrary", "arbitrary", "arbitrary"), 48),
        name="conv_ffn",
    )(h2, wg, wu, wd, cw, cb, st, x1, gt2, gf)


HEADS_PER_GROUP = 4
GROUP_ROWS = HEADS_PER_GROUP * CHUNK
_DELTA_PASSES = dict(kk=1, inv=1, uw=1, qk=1, state=1, intra=1, upd=1)


def _delta_kernel(x_ref, st_ref, cw_ref, ab_ref, hp_ref, s0_ref, z_ref, gb_ref, gdn_ref,
                  o_ref, s_ref, carry, *, nheads):
    c = pl.program_id(1)
    r = GROUP_ROWS
    hd = DK_B

    @pl.when(c == 0)
    def _():
        carry[...] = st_ref[0]
        s_ref[...] = s0_ref[...]

    prev = carry[...]
    carry[...] = x_ref[0, CHUNK - SUBLANES:, :]

    ri = lax.broadcasted_iota(I32, (r, r), 0)
    ci = lax.broadcasted_iota(I32, (r, r), 1)
    same_head = (ri // CHUNK) == (ci // CHUNK)
    incl = same_head & (ri >= ci)
    strict = same_head & (ri > ci)
    triu = jnp.where(same_head & (ri <= ci), 1.0, 0.0).astype(F32)
    eye = jnp.where(ri == ci, 1.0, 0.0).astype(F32)
    row8 = lax.broadcasted_iota(I32, (SUBLANES, r * 2), 0)
    rid = lax.broadcasted_iota(I32, (LANES, r), 0)

    def conv_silu(col0, width):
        xs = x_ref[0, :, col0:col0 + width]
        pv = prev[:, col0:col0 + width]
        w = cw_ref[:, col0:col0 + width]
        y = xs * w[CONVB_WIDTH - 1:CONVB_WIDTH]
        for k in range(1, CONVB_WIDTH):
            rolled = pltpu.roll(xs, k, axis=0)
            head = jnp.where(row8[:, :width] < k, pltpu.roll(pv, k, axis=0), rolled[:SUBLANES])
            shifted = jnp.concatenate([head, rolled[SUBLANES:]], axis=0)
            y = y + shifted * w[CONVB_WIDTH - 1 - k:CONVB_WIDTH - k]
        return _silu(y)

    def l2n(v):
        return v * lax.rsqrt(jnp.sum(v * v, axis=-1, keepdims=True) + EPS)

    def stack(y, fn):
        return jnp.concatenate([fn(y[:, j * hd:(j + 1) * hd]) for j in range(HEADS_PER_GROUP)], axis=0)

    ngroups = nheads // HEADS_PER_GROUP
    gw = HEADS_PER_GROUP * hd
    pp = _DELTA_PASSES

    def prep(g):
        qs = stack(conv_silu(g * gw, gw), lambda v: l2n(v) * (DK_B ** -0.5))
        ks = stack(conv_silu(nheads * hd + g * gw, gw), l2n)
        vs = stack(conv_silu(2 * nheads * hd + g * gw, gw), lambda v: v)
        a_row = ab_ref[0, 0, 0:1, g * r:(g + 1) * r]
        b_row = ab_ref[0, 0, 1:2, g * r:(g + 1) * r]
        alog = hp_ref[0:1, g * r:(g + 1) * r]
        dtb = hp_ref[1:2, g * r:(g + 1) * r]
        xsp = a_row + dtb
        softplus = jnp.maximum(xsp, 0.0) + jnp.log1p(jnp.exp(-jnp.abs(xsp)))
        g_row = -jnp.exp(alog) * softplus
        beta_row = _sigmoid(b_row)
        gc_row = _dot(jnp.broadcast_to(g_row, (SUBLANES, r)), triu, passes=6)[0:1]
        rows = jnp.where(rid == 0, beta_row, jnp.where(rid == 1, gc_row, 0.0))
        cols = rows.T
        beta_col = cols[:, 0:1]
        gc_col = cols[:, 1:2]
        gl_col = jnp.concatenate(
            [jnp.broadcast_to(cols[(j + 1) * CHUNK - 1:(j + 1) * CHUNK, 1:2], (CHUNK, 1))
             for j in range(HEADS_PER_GROUP)], axis=0)
        decay = jnp.where(incl, jnp.exp(gc_col - gc_row), 0.0)
        egc = jnp.exp(gc_col)
        kb = ks * beta_col
        return dict(qs=qs, ks=ks, kb=kb, rhs=jnp.concatenate([vs * beta_col, kb * egc], axis=1),
                    decay=decay, qg=qs * egc, kd=ks * jnp.exp(gl_col - gc_col), gl_col=gl_col)

    gs = [prep(g) for g in range(ngroups)]
    ms = [jnp.where(strict, _dot(d["kb"], d["ks"], _NT, passes=pp["kk"]) * d["decay"], 0.0) for d in gs]

    invs = [eye - jnp.where((ri // 2 == ci // 2), m, 0.0) for m in ms]
    bs = 2
    while bs < CHUNK:
        lower = ((ri // (2 * bs)) == (ci // (2 * bs))) & ((ri % (2 * bs)) >= bs) & ((ci % (2 * bs)) < bs)
        xc = [_dot(inv, jnp.where(lower, m, 0.0), passes=pp["inv"]) for inv, m in zip(invs, ms)]
        invs = [inv - _dot(t, inv, passes=pp["inv"]) for inv, t in zip(invs, xc)]
        bs *= 2

    uws = [_dot(inv, d["rhs"], passes=pp["uw"]) for inv, d in zip(invs, gs)]
    attns = [jnp.where(incl, _dot(d["qs"], d["ks"], _NT, passes=pp["qk"]) * d["decay"], 0.0) for d in gs]

    vnews, qstates = [], []
    for g in range(ngroups):
        vnew, qstate = [], []
        for j in range(HEADS_PER_GROUP):
            sl = slice(j * CHUNK, (j + 1) * CHUNK)
            wq = jnp.concatenate([uws[g][sl, hd:], gs[g]["qg"][sl]], axis=0)
            rs = _dot(wq, s_ref[0, g * HEADS_PER_GROUP + j], passes=pp["state"])
            vnew.append(uws[g][sl, :hd] - rs[:CHUNK])
            qstate.append(rs[CHUNK:])
        vnews.append(vnew)
        qstates.append(qstate)
    intras = [_dot(attns[g], jnp.concatenate(vnews[g], axis=0), passes=pp["intra"]) for g in range(ngroups)]

    for g in range(ngroups):
        for j in range(HEADS_PER_GROUP):
            hidx = g * HEADS_PER_GROUP + j
            sl = slice(j * CHUNK, (j + 1) * CHUNK)
            hs = slice(hidx * hd, (hidx + 1) * hd)
            o = qstates[g][j] + intras[g][sl]
            glast = jnp.exp(gs[g]["gl_col"][j * CHUNK:j * CHUNK + 1, :])
            upd = _dot(gs[g]["kd"][sl], vnews[g][j], _TN, passes=pp["upd"])
            s_ref[0, hidx] = s_ref[0, hidx] * glast + upd
            on = o * lax.rsqrt(jnp.mean(o * o, axis=-1, keepdims=True) + EPS) * gdn_ref[...]
            o_ref[0, :, hs] = on * _silu(z_ref[0, :, hs]) * _sigmoid(gb_ref[0, :, hs])


def _delta(xqkv, st8, cw, ab, hp, s0, z, gb, gdn):
    b, t, cdim = xqkv.shape
    nheads = s0.shape[1]
    dm = z.shape[2]
    n = t // CHUNK
    kern = functools.partial(_delta_kernel, nheads=nheads)
    return pl.pallas_call(
        kern,
        out_shape=(jax.ShapeDtypeStruct((b, t, dm), F32),
                   jax.ShapeDtypeStruct(s0.shape, F32)),
        grid=(b, n),
        in_specs=[pl.BlockSpec((1, CHUNK, cdim), lambda i, c: (i, c, 0)),
                  pl.BlockSpec((1, SUBLANES, cdim), lambda i, c: (i, 0, 0)),
                  pl.BlockSpec((CONVB_WIDTH, cdim), lambda i, c: (0, 0)),
                  pl.BlockSpec((1, 1, SUBLANES, nheads * CHUNK), lambda i, c: (i, c, 0, 0)),
                  pl.BlockSpec((SUBLANES, nheads * CHUNK), lambda i, c: (0, 0)),
                  pl.BlockSpec((1, nheads, DK_B, DV_B), lambda i, c: (i, 0, 0, 0)),
                  pl.BlockSpec((1, CHUNK, dm), lambda i, c: (i, c, 0)),
                  pl.BlockSpec((1, CHUNK, dm), lambda i, c: (i, c, 0)),
                  pl.BlockSpec((1, DV_B), lambda i, c: (0, 0))],
        out_specs=(pl.BlockSpec((1, CHUNK, dm), lambda i, c: (i, c, 0)),
                   pl.BlockSpec((1, nheads, DK_B, DV_B), lambda i, c: (i, 0, 0, 0))),
        scratch_shapes=[pltpu.VMEM((SUBLANES, cdim), F32)],
        compiler_params=_cparams(("arbitrary", "arbitrary"), 40),
        name="gated_delta",
    )(xqkv, st8, cw, ab, hp, s0, z, gb, gdn)


_T5_THRESHOLDS = tuple(math.ceil(8 * 2 ** (k / 2)) for k in range(1, 8))
_FAR_BUCKET = NUM_BUCKETS // 2 - 1


MASKED = -1e30


def _attn_kernel(itab, jtab, rb_ref, q_ref, qi_ref, sm_ref, kit_ref, ka_ref, kb_ref, va_ref, vb_ref, o_ref,
                 skey, thr, jcut, qis, wib, madd, m_scr, l_scr, acc, dtile,
                 *, past, qb, tk, rb, n_sel, nheads):
    b = pl.program_id(0)
    step = pl.program_id(1)
    i = itab[step]
    j = jtab[step]
    hd = HD_A
    rep = tk // LANES
    kd = past // tk + i * (qb // tk)
    q0 = past + i * qb
    blk_b = kd - 2 * j
    blk_a = blk_b - 1

    def lanes(v):
        return jnp.concatenate([v] * rep, axis=1) if rep > 1 else v

    @pl.when((b == 0) & (step == 0))
    def _():
        rr = lax.broadcasted_iota(I32, (qb, 2 * tk), 0)
        cc = lax.broadcasted_iota(I32, (qb, 2 * tk), 1)
        rel = cc - tk - rr
        n = jnp.abs(rel)
        large = jnp.full((qb, 2 * tk), NUM_BUCKETS // 4, I32)
        for th in _T5_THRESHOLDS:
            large = large + jnp.where(n >= th, 1, 0)
        bucket = jnp.where(rel > 0, NUM_BUCKETS // 2, 0) + jnp.where(n < NUM_BUCKETS // 4, n, large)

        def head_body(h, _):
            base = rb_ref[_FAR_BUCKET, h]
            tile = jnp.zeros((qb, 2 * tk), F32)
            for bk in range(NUM_BUCKETS):
                if bk != _FAR_BUCKET:
                    tile = jnp.where(bucket == bk, (rb_ref[bk, h] - base) * LOG2E, tile)
            dtile[h] = tile
            return 0

        lax.fori_loop(0, nheads, head_body, 0)

    @pl.when(j == 0)
    def _():
        sm = sm_ref[0]
        for h in range(H_I):
            qis[h] = qi_ref[0, :, h * D_IDX:(h + 1) * D_IDX]
            wcol = sm[:, D_IDX + h:D_IDX + h + 1] * ((H_I * D_IDX) ** -0.5)
            wib[h] = jnp.broadcast_to(wcol, (qb, LANES))
        m_scr[...] = jnp.full(m_scr.shape, MASKED, F32)
        l_scr[...] = jnp.zeros(l_scr.shape, F32)
        acc[...] = jnp.zeros(acc.shape, F32)

        qpos = q0 + lax.broadcasted_iota(I32, (qb, tk), 0)
        lim = (qpos // CHUNK + 1) * CHUNK
        lane = lax.broadcasted_iota(I32, (qb, tk), 1)

        def score_tile(j, _):
            off = pl.multiple_of(j * tk, tk)
            kit = kit_ref[0, :, pl.ds(off, tk)]
            sc = jnp.zeros((qb, tk), F32)
            for h in range(H_I):
                d = jnp.dot(qis[h], kit, preferred_element_type=F32)
                sc = sc + jnp.maximum(d, 0.0) * lanes(wib[h])
            bits = pltpu.bitcast(sc + 0.0, I32)
            key = bits ^ ((bits >> 31) & 0x7FFFFFFF)
            key = jnp.where(lane + off < lim, key, INT_MIN)
            skey[:, pl.ds(off, tk)] = key
            return 0

        lax.fori_loop(0, kd + 1, score_tile, 0)

        def select_rows(rblk, _):
            r0 = pl.multiple_of(rblk * rb, rb)
            rlane = lax.broadcasted_iota(I32, (rb, tk), 1)

            def count(pred):
                def body(jj, c):
                    off = pl.multiple_of(jj * tk, tk)
                    mk = jnp.where(pred(skey[pl.ds(r0, rb), pl.ds(off, tk)], off), 1, 0)
                    part = mk[:, :LANES]
                    for u in range(1, rep):
                        part = part + mk[:, u * LANES:(u + 1) * LANES]
                    return c + part
                c = lax.fori_loop(0, kd + 1, body, jnp.zeros((rb, LANES), I32))
                return jnp.broadcast_to(jnp.sum(c, axis=1, keepdims=True), (rb, LANES))

            def bit_body(bi, tu):
                cand_u = tu | lax.shift_left(jnp.int32(1), jnp.int32(31) - bi)
                cand_s = lanes(cand_u ^ INT_MIN)
                cnt = count(lambda x, off: x >= cand_s)
                return jnp.where(cnt >= n_sel, cand_u, tu)

            tu = lax.fori_loop(0, 32, bit_body, jnp.zeros((rb, LANES), I32))
            ts = jnp.maximum(tu ^ INT_MIN, INT_MIN + 1)
            thr[pl.ds(r0, rb), :] = ts
            jcut[pl.ds(r0, rb), :] = jnp.full((rb, LANES), 2 ** 30, I32)

            tsl = lanes(ts)
            c_gt = count(lambda x, off: x > tsl)
            c_eq = count(lambda x, off: x == tsl)
            need = n_sel - c_gt
            excess = jnp.max(jnp.where((c_eq > need) & (tu != 0), 1.0, 0.0))

            @pl.when(excess > 0.5)
            def _():
                def jbit(bi, jc):
                    cand = jc | lax.shift_left(jnp.int32(1), jnp.int32(15) - bi)
                    cl = lanes(cand)
                    cnt = count(lambda x, off: (x == tsl) & (rlane + off < cl))
                    return jnp.where(cnt <= need, cand, jc)
                jcut[pl.ds(r0, rb), :] = lax.fori_loop(0, 16, jbit, jnp.zeros((rb, LANES), I32))

            return 0

        lax.fori_loop(0, qb // rb, select_rows, 0)

    offa = pl.multiple_of(jnp.maximum(blk_a, 0) * tk, tk)
    offb = pl.multiple_of(blk_b * tk, tk)
    ts = lanes(thr[...])
    jc = lanes(jcut[...])
    lane = lax.broadcasted_iota(I32, (qb, tk), 1)
    hang = jnp.where(blk_a < 0, MASKED, 0.0)
    for off, col0, extra in ((offa, 0, hang), (offb, tk, 0.0)):
        x = skey[:, pl.ds(off, tk)]
        sel = (x > ts) | ((x == ts) & (lane + off < jc))
        madd[:, col0:col0 + tk] = jnp.where(sel, 0.0, MASKED) + extra

    def flash(near):
        for h in range(nheads):
            hs = slice(h * hd, (h + 1) * hd)
            qh = q_ref[0, :, hs]
            s = jnp.concatenate(
                [lax.dot_general(qh, ka_ref[0, :, hs], _NT, preferred_element_type=F32),
                 lax.dot_general(qh, kb_ref[0, :, hs], _NT, preferred_element_type=F32)], axis=1)
            s = s + madd[...]
            if near:
                s = s + dtile[h]
            m_prev = m_scr[h]
            m_new = jnp.maximum(m_prev, jnp.max(s, axis=1, keepdims=True))
            alpha = jnp.exp2(m_prev - m_new)
            p = jnp.exp2(s - jnp.concatenate([m_new] * (2 * rep), axis=1))
            l_scr[h] = alpha * l_scr[h] + jnp.sum(p, axis=1, keepdims=True)
            m_scr[h] = m_new
            pb = p.astype(BF16)
            pv = (jnp.dot(pb[:, :tk], va_ref[0, :, hs], preferred_element_type=F32)
                  + jnp.dot(pb[:, tk:], vb_ref[0, :, hs], preferred_element_type=F32))
            acc[:, hs] = alpha * acc[:, hs] + pv

    @pl.when(j == 0)
    def _():
        flash(True)

    @pl.when(j > 0)
    def _():
        flash(False)

    @pl.when(j == kd // 2)
    def _():
        for h in range(nheads):
            hs = slice(h * hd, (h + 1) * hd)
            o_ref[0, :, hs] = (acc[:, hs] / l_scr[h]).astype(o_ref.dtype)


def _attention(rel_bias, q, qi, small, kit, kall, vall, *, past, n_sel):
    b, t, dm = q.shape
    lp = kall.shape[1]
    nheads = dm // HD_A
    qb = min(t, 256)
    tk = 256
    nq = t // qb
    assert past % tk == 0 and lp % tk == 0 and t % qb == 0
    assert qb == tk or nq == 1
    assert lp < 2 ** 16
    kdiag = lambda i: past // tk + i * (qb // tk)
    steps = [(i, j) for i in range(nq) for j in range(kdiag(i) // 2 + 1)]
    itab = jnp.asarray([s[0] for s in steps], I32)
    jtab = jnp.asarray([s[1] for s in steps], I32)
    kern = functools.partial(_attn_kernel, past=past, qb=qb, tk=tk, rb=min(qb, 128), n_sel=n_sel, nheads=nheads)

    def row(bi, s, it, jt):
        return (bi, it[s], 0)

    def tile_a(bi, s, it, jt):
        return (bi, jnp.maximum(kdiag(it[s]) - 2 * jt[s] - 1, 0), 0)

    def tile_b(bi, s, it, jt):
        return (bi, kdiag(it[s]) - 2 * jt[s], 0)

    grid_spec = pltpu.PrefetchScalarGridSpec(
        num_scalar_prefetch=2,
        grid=(b, len(steps)),
        in_specs=[pl.BlockSpec(memory_space=pltpu.SMEM),
                  pl.BlockSpec((1, qb, dm), row),
                  pl.BlockSpec((1, qb, H_I * D_IDX), row),
                  pl.BlockSpec((1, qb, LANES), row),
                  pl.BlockSpec((1, D_IDX, lp), lambda bi, s, it, jt: (bi, 0, 0)),
                  pl.BlockSpec((1, tk, dm), tile_a),
                  pl.BlockSpec((1, tk, dm), tile_b),
                  pl.BlockSpec((1, tk, dm), tile_a),
                  pl.BlockSpec((1, tk, dm), tile_b)],
        out_specs=pl.BlockSpec((1, qb, dm), row),
        scratch_shapes=[pltpu.VMEM((qb, lp), I32),
                        pltpu.VMEM((qb, LANES), I32),
                        pltpu.VMEM((qb, LANES), I32),
                        pltpu.VMEM((H_I, qb, D_IDX), BF16),
                        pltpu.VMEM((H_I, qb, LANES), F32),
                        pltpu.VMEM((qb, 2 * tk), F32),
                        pltpu.VMEM((nheads, qb, LANES), F32),
                        pltpu.VMEM((nheads, qb, LANES), F32),
                        pltpu.VMEM((qb, dm), F32),
                        pltpu.VMEM((nheads, qb, 2 * tk), F32)])
    return pl.pallas_call(
        kern,
        out_shape=jax.ShapeDtypeStruct((b, t, dm), F32),
        grid_spec=grid_spec,
        compiler_params=_cparams(("arbitrary", "arbitrary"), 60),
        name="sparse_attn",
    )(itab, jtab, rel_bias, q, qi, small, kit, kall, kall, vall, vall)


def _pad_rows(a, rows):
    return jnp.pad(a, [(0, 0), (rows - a.shape[1], 0)] + [(0, 0)] * (a.ndim - 2))


def _run_group(x, mod, cache_k, cache_v, cache_kidx, st_conv, st_delta, st_ffn, rel_bias, g_final, w):
    b, t, d = x.shape
    nheads = d // HD_A
    sh1, sc1, gt1, sh2, sc2, gt2 = [m[:, None, :] for m in jnp.split(mod, 6, axis=-1)]
    m_rows = b * t

    h = _norm_mod(x, w["g_norm1"], sc1, sh1).reshape(m_rows, d)
    proj = lambda name, dt=F32: _matmul(h, w[name], dt, name="proj_" + name).reshape(b, t, -1)
    qa, qi = proj("w_qa", BF16), proj("w_qi", BF16)
    ka, va = proj("w_ka"), proj("w_va")
    small = proj("w_small")
    qkv_b, z_b, gate_a, gate_b = proj("w_qkvb"), proj("w_z"), proj("w_ga"), proj("w_gb")
    ki = small[..., :D_IDX]

    past = 0 if cache_k is None else cache_k.shape[1]
    l_all = past + t
    tk = 256
    lp = -(-l_all // tk) * tk
    if cache_k is None:
        k_all, v_all, ki_all = ka, va, ki
    else:
        k_all = jnp.concatenate([cache_k.reshape(b, past, d), ka], axis=1)
        v_all = jnp.concatenate([cache_v.reshape(b, past, d), va], axis=1)
        ki_all = jnp.concatenate([cache_kidx, ki], axis=1)
    padk = lambda a: jnp.pad(a, ((0, 0), (0, lp - l_all), (0, 0))).astype(BF16)
    k_all, v_all = padk(k_all), padk(v_all)
    kit = jnp.swapaxes(padk(ki_all), 1, 2)
    n_sel = min(TOPK_MAX, l_all // 4)
    o_a = _attention(rel_bias, qa, qi, small, kit, k_all, v_all, past=past, n_sel=n_sel)

    n = t // CHUNK
    to_rows = lambda v: jnp.swapaxes(v.reshape(b, n, CHUNK, nheads), 2, 3).reshape(b, n, 1, nheads * CHUNK)
    ab = jnp.concatenate([to_rows(small[..., 80:96]), to_rows(small[..., 96:112]),
                          jnp.zeros((b, n, SUBLANES - 2, nheads * CHUNK), F32)], axis=2)
    ogb, new_delta = _delta(qkv_b, _pad_rows(st_conv, SUBLANES), w["conv_b_w"], ab, w["hp_rows"],
                            st_delta, z_b, gate_b, w["g_delta_norm"])
    assert t >= CONVB_WIDTH - 1
    new_conv = qkv_b[:, t - (CONVB_WIDTH - 1):]

    x1 = _merge_out(o_a, gate_a, ogb, w["w_out"], x, gt1)
    h2 = _norm_mod(x1, w["g_norm2"], sc2, sh2)
    y, tail = _ffn(h2, w["w_ffn_gate"], w["w_ffn_up"], w["w_ffn_down"], w["ffn_conv_w"], w["ffn_conv_b"],
                   st_ffn, x1, gt2, g_final)
    new_ffn = tail[:, -1, SUBLANES - (FFN_CONV_WIDTH - 1):]
    kshape = (1, b, t, nheads, HD_A)
    return (y, ka.reshape(kshape), va.reshape(kshape), ki[None], new_conv[None], new_delta[None], new_ffn[None])


def kernel(x_prompt, x_sample, c_prompt, c_sample, cache_k, cache_v, cache_kidx, state_conv_b, state_delta,
           state_ffn_conv, rel_bias, g_final, w_ada, b_ada, g_norm1, w_in, conv_b_w, a_log, dt_bias,
           g_delta_norm, w_out, g_norm2, w_ffn_gate, ffn_conv_w, ffn_conv_b, w_ffn_up, w_ffn_down):
    assert w_ada.shape[0] == 1, "one layer"
    d = x_prompt.shape[-1]
    nheads = d // HD_A
    bp, bs = x_prompt.shape[0], x_sample.shape[0]
    convb = 3 * nheads * DK_B

    wi_ = w_in[0]
    sizes = (d, d, d, H_I * D_IDX, D_IDX, H_I, convb, nheads, nheads, d, d, d)
    offs = [0]
    for s in sizes:
        offs.append(offs[-1] + s)
    col = lambda k: wi_[:, offs[k]:offs[k + 1]]
    w_small = jnp.concatenate([col(4), col(5), col(7), col(8),
                               jnp.zeros((d, LANES - D_IDX - H_I - 2 * nheads), F32)], axis=1)
    bf = lambda a: a.astype(BF16)
    w = dict(
        w_qa=bf(col(0) * (HD_A ** -0.5 * LOG2E)), w_ka=bf(col(1)), w_va=bf(col(2)), w_qi=bf(col(3)), w_small=bf(w_small),
        w_qkvb=bf(col(6)), w_z=bf(col(9)), w_ga=bf(col(10)), w_gb=bf(col(11)),
        g_norm1=g_norm1, g_norm2=g_norm2, conv_b_w=conv_b_w[0], g_delta_norm=g_delta_norm,
        w_out=bf(w_out[0]), w_ffn_gate=bf(w_ffn_gate[0]), w_ffn_up=bf(w_ffn_up[0]),
        w_ffn_down=bf(w_ffn_down[0]), ffn_conv_w=ffn_conv_w[0], ffn_conv_b=ffn_conv_b,
        hp_rows=jnp.concatenate([jnp.repeat(a_log[0], CHUNK)[None], jnp.repeat(dt_bias[0], CHUNK)[None],
                                 jnp.zeros((SUBLANES - 2, nheads * CHUNK), F32)], axis=0),
    )

    c_all = jnp.concatenate([c_prompt, c_sample], axis=0)
    rows = -(-c_all.shape[0] // SUBLANES) * SUBLANES
    c_all = jnp.pad(c_all, ((0, rows - c_all.shape[0]), (0, 0)))
    mod = _adaln(c_all, w_ada[0], b_ada)

    zeros = lambda *s: jnp.zeros(s, F32)
    outs_p = _run_group(x_prompt, mod[:bp], None, None, None,
                        zeros(bp, CONVB_WIDTH - 1, convb), zeros(bp, nheads, DK_B, DV_B),
                        zeros(bp, FFN_CONV_WIDTH - 1, w_ffn_gate.shape[-1]), rel_bias, g_final[None], w)
    outs_s = _run_group(x_sample, mod[bp:bp + bs], cache_k[0], cache_v[0], cache_kidx[0],
                        state_conv_b[0], state_delta[0], state_ffn_conv[0], rel_bias, g_final[None], w)
    return (outs_p[0], outs_s[0]) + tuple(outs_p[1:]) + tuple(outs_s[1:])
```

```python
import functools
import math

import jax
import jax.numpy as jnp
from jax import lax
from jax.experimental import pallas as pl
from jax.experimental.pallas import tpu as pltpu

F32 = jnp.float32
BF16 = jnp.bfloat16
I32 = jnp.int32

CHUNK = 64
HD_A = 128
H_I = 16
D_IDX = 64
TOPK_MAX = 256
NUM_BUCKETS = 32
MAX_DISTANCE = 128
DK_B = 128
DV_B = 128
CONVB_WIDTH = 4
FFN_CONV_WIDTH = 3
EPS = 1e-6
LOG2E = math.log2(math.e)

V7X_VMEM_BYTES = 64 * 1024 * 1024
LANES = 128
SUBLANES = 8
INT_MIN = -(2 ** 31)


def _cparams(sem, vmem_mb):
    assert vmem_mb * 1024 * 1024 < V7X_VMEM_BYTES
    return pltpu.CompilerParams(dimension_semantics=sem, vmem_limit_bytes=vmem_mb * 1024 * 1024)


def _sigmoid(x):
    return 1.0 / (1.0 + jnp.exp(-x))


def _silu(x):
    return x * _sigmoid(x)


def _dot(a, b, dims=None, passes=1):
    dn = dims if dims is not None else (((a.ndim - 1,), (0,)), ((), ()))
    if passes == 6:
        return lax.dot_general(a, b, dn, precision=lax.Precision.HIGHEST, preferred_element_type=F32)
    ah = a.astype(BF16)
    bh = b.astype(BF16)
    out = lax.dot_general(ah, bh, dn, preferred_element_type=F32)
    if passes == 3:
        al = (a - ah.astype(F32)).astype(BF16)
        bl = (b - bh.astype(F32)).astype(BF16)
        out = out + lax.dot_general(ah, bl, dn, preferred_element_type=F32)
        out = out + lax.dot_general(al, bh, dn, preferred_element_type=F32)
    return out


_NT = (((1,), (1,)), ((), ()))
_TN = (((0,), (0,)), ((), ()))


def _adaln_kernel(c_ref, w_ref, b_ref, o_ref):
    o_ref[...] = _dot(_silu(c_ref[...]), w_ref[...]) + b_ref[...]


def _adaln(c, w, b):
    bp, d = c.shape
    n = w.shape[1]
    tn = 1024
    return pl.pallas_call(
        _adaln_kernel,
        out_shape=jax.ShapeDtypeStruct((bp, n), F32),
        grid=(n // tn,),
        in_specs=[pl.BlockSpec((bp, d), lambda j: (0, 0)),
                  pl.BlockSpec((d, tn), lambda j: (0, j)),
                  pl.BlockSpec((1, tn), lambda j: (0, j))],
        out_specs=pl.BlockSpec((bp, tn), lambda j: (0, j)),
        compiler_params=_cparams(("arbitrary",), 40),
        name="adaln",
    )(c, w, b)


def _norm_mod_kernel(x_ref, g_ref, sc_ref, sh_ref, o_ref):
    x = x_ref[...]
    y = x * lax.rsqrt(jnp.mean(x * x, axis=-1, keepdims=True) + EPS)
    o_ref[...] = ((y * g_ref[...]) * (1.0 + sc_ref[...]) + sh_ref[...]).astype(o_ref.dtype)


def _row_tiles(b, t, rows):
    if t >= rows:
        return 1, rows
    bb = max(1, min(b, rows // t))
    while b % bb:
        bb -= 1
    return bb, t


def _norm_mod(x, g, sc, sh):
    b, t, d = x.shape
    bb, tt = _row_tiles(b, t, 512)
    return pl.pallas_call(
        _norm_mod_kernel,
        out_shape=jax.ShapeDtypeStruct((b, t, d), BF16),
        grid=(b // bb, t // tt),
        in_specs=[pl.BlockSpec((bb, tt, d), lambda i, j: (i, j, 0)),
                  pl.BlockSpec((1, d), lambda i, j: (0, 0)),
                  pl.BlockSpec((bb, 1, d), lambda i, j: (i, 0, 0)),
                  pl.BlockSpec((bb, 1, d), lambda i, j: (i, 0, 0))],
        out_specs=pl.BlockSpec((bb, tt, d), lambda i, j: (i, j, 0)),
        compiler_params=_cparams(("arbitrary", "arbitrary"), 32),
        name="norm_mod",
    )(x, g, sc, sh)


def _mm_kernel(a_ref, w_ref, o_ref):
    o_ref[...] = jnp.dot(a_ref[...], w_ref[...], preferred_element_type=F32).astype(o_ref.dtype)


def _matmul(a, w, out_dtype=F32, name="proj"):
    m, k = a.shape
    n = w.shape[1]
    tm = min(m, 1024)
    tn = min(n, 512)
    return pl.pallas_call(
        _mm_kernel,
        out_shape=jax.ShapeDtypeStruct((m, n), out_dtype),
        grid=(m // tm, n // tn),
        in_specs=[pl.BlockSpec((tm, k), lambda i, j: (i, 0)),
                  pl.BlockSpec((k, tn), lambda i, j: (0, j))],
        out_specs=pl.BlockSpec((tm, tn), lambda i, j: (i, j)),
        compiler_params=_cparams(("arbitrary", "arbitrary"), 40),
        name=name,
    )(a, w)


def _merge_out_kernel(oa_ref, ga_ref, ogb_ref, w_ref, x_ref, gt_ref, o_ref, mixed):
    bb, tt, k = oa_ref.shape

    @pl.when(pl.program_id(2) == 0)
    def _():
        mix = _sigmoid(ga_ref[...]) * oa_ref[...] + ogb_ref[...]
        mixed[...] = mix.reshape(bb * tt, k).astype(BF16)

    y = jnp.dot(mixed[...], w_ref[...], preferred_element_type=F32)
    o_ref[...] = x_ref[...] + gt_ref[...] * y.reshape(bb, tt, -1)


def _merge_out(oa, ga, ogb, w, x, gt):
    b, t, k = oa.shape
    n = w.shape[1]
    bb, tt = _row_tiles(b, t, 256)
    tn = 512
    row = lambda i, s, j: (i, s, 0)
    return pl.pallas_call(
        _merge_out_kernel,
        out_shape=jax.ShapeDtypeStruct((b, t, n), F32),
        grid=(b // bb, t // tt, n // tn),
        in_specs=[pl.BlockSpec((bb, tt, k), row),
                  pl.BlockSpec((bb, tt, k), row),
                  pl.BlockSpec((bb, tt, k), row),
                  pl.BlockSpec((k, tn), lambda i, s, j: (0, j)),
                  pl.BlockSpec((bb, tt, tn), lambda i, s, j: (i, s, j)),
                  pl.BlockSpec((bb, 1, tn), lambda i, s, j: (i, 0, j))],
        out_specs=pl.BlockSpec((bb, tt, tn), lambda i, s, j: (i, s, j)),
        scratch_shapes=[pltpu.VMEM((bb * tt, k), BF16)],
        compiler_params=_cparams(("arbitrary", "arbitrary", "arbitrary"), 40),
        name="merge_out",
    )(oa, ga, ogb, w, x, gt)


def _ffn_kernel(h_ref, wg_ref, wu_ref, wd_ref, cw_ref, cb_ref, st_ref, x_ref, gt_ref, gf_ref,
                y_ref, tail_ref, acc, carry):
    s = pl.program_id(1)
    f = pl.program_id(2)
    nf = pl.num_programs(2)
    bb, tt, d = h_ref.shape
    tm = bb * tt
    tf = wg_ref.shape[1]

    h = h_ref[...].reshape(tm, d)
    gate = jnp.dot(h, wg_ref[...], preferred_element_type=F32)
    up = jnp.dot(h, wu_ref[...], preferred_element_type=F32)

    st = st_ref[...]
    first = s == 0

    @pl.when(first)
    def _():
        carry[f] = jnp.zeros((SUBLANES, tf), F32)

    prev = carry[f]
    p0 = jnp.where(first, st[:, 0:1, :], prev[6:7, :][None])
    p1 = jnp.where(first, st[:, 1:2, :], prev[7:8, :][None])
    p0 = jnp.broadcast_to(p0, (bb, tt, tf)).reshape(tm, tf)
    p1 = jnp.broadcast_to(p1, (bb, tt, tf)).reshape(tm, tf)
    tpos = lax.broadcasted_iota(I32, (tm, tf), 0) % tt
    r1 = pltpu.roll(gate, 1, axis=0)
    r2 = pltpu.roll(gate, 2, axis=0)
    x1 = jnp.where(tpos == 0, p1, r1)
    x2 = jnp.where(tpos == 0, p0, jnp.where(tpos == 1, p1, r2))
    cw = cw_ref[...]
    a = cw[0:1] * x2 + cw[1:2] * x1 + cw[2:3] * gate + cb_ref[...]
    act = (_silu(a) * up).astype(BF16)
    contrib = jnp.dot(act, wd_ref[...], preferred_element_type=F32)

    g3 = gate.reshape(bb, tt, tf)
    tail_ref[...] = g3[:, None, tt - SUBLANES:, :]
    carry[f] = gate[tm - SUBLANES:, :]

    @pl.when(f == 0)
    def _():
        acc[...] = contrib

    @pl.when(f > 0)
    def _():
        acc[...] += contrib

    @pl.when(f == nf - 1)
    def _():
        x = x_ref[...] + gt_ref[...] * acc[...].reshape(bb, tt, d)
        y = x * lax.rsqrt(jnp.mean(x * x, axis=-1, keepdims=True) + EPS)
        y_ref[...] = y * gf_ref[...]


def _ffn(h2, wg, wu, wd, cw, cb, st, x1, gt2, gf):
    b, t, d = h2.shape
    dff = wg.shape[1]
    bb, tt = _row_tiles(b, t, 512)
    tf = 512
    nf = dff // tf
    row = lambda i, s, f: (i, s, 0)
    return pl.pallas_call(
        _ffn_kernel,
        out_shape=(jax.ShapeDtypeStruct((b, t, d), F32),
                   jax.ShapeDtypeStruct((b, t // tt, SUBLANES, dff), F32)),
        grid=(b // bb, t // tt, nf),
        in_specs=[pl.BlockSpec((bb, tt, d), row),
                  pl.BlockSpec((d, tf), lambda i, s, f: (0, f)),
                  pl.BlockSpec((d, tf), lambda i, s, f: (0, f)),
                  pl.BlockSpec((tf, d), lambda i, s, f: (f, 0)),
                  pl.BlockSpec((FFN_CONV_WIDTH, tf), lambda i, s, f: (0, f)),
                  pl.BlockSpec((1, tf), lambda i, s, f: (0, f)),
                  pl.BlockSpec((bb, FFN_CONV_WIDTH - 1, tf), lambda i, s, f: (i, 0, f)),
                  pl.BlockSpec((bb, tt, d), row),
                  pl.BlockSpec((bb, 1, d), lambda i, s, f: (i, 0, 0)),
                  pl.BlockSpec((1, d), lambda i, s, f: (0, 0))],
        out_specs=(pl.BlockSpec((bb, tt, d), row),
                   pl.BlockSpec((bb, 1, SUBLANES, tf), lambda i, s, f: (i, s, 0, f))),
        scratch_shapes=[pltpu.VMEM((bb * tt, d), F32),
                        pltpu.VMEM((nf, SUBLANES, tf), F32)],
        compiler_params=_cparams(("arbitrary", "arbitrary", "arbitrary"), 48),
        name="conv_ffn",
    )(h2, wg, wu, wd, cw, cb, st, x1, gt2, gf)


HEADS_PER_GROUP = 4
GROUP_ROWS = HEADS_PER_GROUP * CHUNK
_DELTA_PASSES = dict(kk=1, inv=1, uw=1, qk=1, state=1, intra=1, upd=1)


def _delta_kernel(x_ref, st_ref, cw_ref, ab_ref, hp_ref, s0_ref, z_ref, gb_ref, gdn_ref,
                  o_ref, s_ref, carry, *, nheads):
    c = pl.program_id(1)
    r = GROUP_ROWS
    hd = DK_B

    @pl.when(c == 0)
    def _():
        carry[...] = st_ref[0]
        s_ref[...] = s0_ref[...]

    prev = carry[...]
    carry[...] = x_ref[0, CHUNK - SUBLANES:, :]

    ri = lax.broadcasted_iota(I32, (r, r), 0)
    ci = lax.broadcasted_iota(I32, (r, r), 1)
    same_head = (ri // CHUNK) == (ci // CHUNK)
    incl = same_head & (ri >= ci)
    strict = same_head & (ri > ci)
    triu = jnp.where(same_head & (ri <= ci), 1.0, 0.0).astype(F32)
    eye = jnp.where(ri == ci, 1.0, 0.0).astype(F32)
    row8 = lax.broadcasted_iota(I32, (SUBLANES, r * 2), 0)
    rid = lax.broadcasted_iota(I32, (LANES, r), 0)

    def conv_silu(col0, width):
        xs = x_ref[0, :, col0:col0 + width]
        pv = prev[:, col0:col0 + width]
        w = cw_ref[:, col0:col0 + width]
        y = xs * w[CONVB_WIDTH - 1:CONVB_WIDTH]
        for k in range(1, CONVB_WIDTH):
            rolled = pltpu.roll(xs, k, axis=0)
            head = jnp.where(row8[:, :width] < k, pltpu.roll(pv, k, axis=0), rolled[:SUBLANES])
            shifted = jnp.concatenate([head, rolled[SUBLANES:]], axis=0)
            y = y + shifted * w[CONVB_WIDTH - 1 - k:CONVB_WIDTH - k]
        return _silu(y)

    def l2n(v):
        return v * lax.rsqrt(jnp.sum(v * v, axis=-1, keepdims=True) + EPS)

    def stack(y, fn):
        return jnp.concatenate([fn(y[:, j * hd:(j + 1) * hd]) for j in range(HEADS_PER_GROUP)], axis=0)

    ngroups = nheads // HEADS_PER_GROUP
    gw = HEADS_PER_GROUP * hd
    pp = _DELTA_PASSES

    def prep(g):
        qs = stack(conv_silu(g * gw, gw), lambda v: l2n(v) * (DK_B ** -0.5))
        ks = stack(conv_silu(nheads * hd + g * gw, gw), l2n)
        vs = stack(conv_silu(2 * nheads * hd + g * gw, gw), lambda v: v)
        a_row = ab_ref[0, 0, 0:1, g * r:(g + 1) * r]
        b_row = ab_ref[0, 0, 1:2, g * r:(g + 1) * r]
        alog = hp_ref[0:1, g * r:(g + 1) * r]
        dtb = hp_ref[1:2, g * r:(g + 1) * r]
        xsp = a_row + dtb
        softplus = jnp.maximum(xsp, 0.0) + jnp.log1p(jnp.exp(-jnp.abs(xsp)))
        g_row = -jnp.exp(alog) * softplus
        beta_row = _sigmoid(b_row)
        gc_row = _dot(jnp.broadcast_to(g_row, (SUBLANES, r)), triu, passes=6)[0:1]
        rows = jnp.where(rid == 0, beta_row, jnp.where(rid == 1, gc_row, 0.0))
        cols = rows.T
        beta_col = cols[:, 0:1]
        gc_col = cols[:, 1:2]
        gl_col = jnp.concatenate(
            [jnp.broadcast_to(cols[(j + 1) * CHUNK - 1:(j + 1) * CHUNK, 1:2], (CHUNK, 1))
             for j in range(HEADS_PER_GROUP)], axis=0)
        decay = jnp.where(incl, jnp.exp(gc_col - gc_row), 0.0)
        egc = jnp.exp(gc_col)
        kb = ks * beta_col
        return dict(qs=qs, ks=ks, kb=kb, rhs=jnp.concatenate([vs * beta_col, kb * egc], axis=1),
                    decay=decay, qg=qs * egc, kd=ks * jnp.exp(gl_col - gc_col), gl_col=gl_col)

    gs = [prep(g) for g in range(ngroups)]
    ms = [jnp.where(strict, _dot(d["kb"], d["ks"], _NT, passes=pp["kk"]) * d["decay"], 0.0) for d in gs]

    invs = [eye - jnp.where((ri // 2 == ci // 2), m, 0.0) for m in ms]
    bs = 2
    while bs < CHUNK:
        lower = ((ri // (2 * bs)) == (ci // (2 * bs))) & ((ri % (2 * bs)) >= bs) & ((ci % (2 * bs)) < bs)
        xc = [_dot(inv, jnp.where(lower, m, 0.0), passes=pp["inv"]) for inv, m in zip(invs, ms)]
        invs = [inv - _dot(t, inv, passes=pp["inv"]) for inv, t in zip(invs, xc)]
        bs *= 2

    uws = [_dot(inv, d["rhs"], passes=pp["uw"]) for inv, d in zip(invs, gs)]
    attns = [jnp.where(incl, _dot(d["qs"], d["ks"], _NT, passes=pp["qk"]) * d["decay"], 0.0) for d in gs]

    vnews, qstates = [], []
    for g in range(ngroups):
        vnew, qstate = [], []
        for j in range(HEADS_PER_GROUP):
            sl = slice(j * CHUNK, (j + 1) * CHUNK)
            wq = jnp.concatenate([uws[g][sl, hd:], gs[g]["qg"][sl]], axis=0)
            rs = _dot(wq, s_ref[0, g * HEADS_PER_GROUP + j], passes=pp["state"])
            vnew.append(uws[g][sl, :hd] - rs[:CHUNK])
            qstate.append(rs[CHUNK:])
        vnews.append(vnew)
        qstates.append(qstate)
    intras = [_dot(attns[g], jnp.concatenate(vnews[g], axis=0), passes=pp["intra"]) for g in range(ngroups)]

    for g in range(ngroups):
        for j in range(HEADS_PER_GROUP):
            hidx = g * HEADS_PER_GROUP + j
            sl = slice(j * CHUNK, (j + 1) * CHUNK)
            hs = slice(hidx * hd, (hidx + 1) * hd)
            o = qstates[g][j] + intras[g][sl]
            glast = jnp.exp(gs[g]["gl_col"][j * CHUNK:j * CHUNK + 1, :])
            upd = _dot(gs[g]["kd"][sl], vnews[g][j], _TN, passes=pp["upd"])
            s_ref[0, hidx] = s_ref[0, hidx] * glast + upd
            on = o * lax.rsqrt(jnp.mean(o * o, axis=-1, keepdims=True) + EPS) * gdn_ref[...]
            o_ref[0, :, hs] = on * _silu(z_ref[0, :, hs]) * _sigmoid(gb_ref[0, :, hs])


def _delta(xqkv, st8, cw, ab, hp, s0, z, gb, gdn):
    b, t, cdim = xqkv.shape
    nheads = s0.shape[1]
    dm = z.shape[2]
    n = t // CHUNK
    kern = functools.partial(_delta_kernel, nheads=nheads)
    return pl.pallas_call(
        kern,
        out_shape=(jax.ShapeDtypeStruct((b, t, dm), F32),
                   jax.ShapeDtypeStruct(s0.shape, F32)),
        grid=(b, n),
        in_specs=[pl.BlockSpec((1, CHUNK, cdim), lambda i, c: (i, c, 0)),
                  pl.BlockSpec((1, SUBLANES, cdim), lambda i, c: (i, 0, 0)),
                  pl.BlockSpec((CONVB_WIDTH, cdim), lambda i, c: (0, 0)),
                  pl.BlockSpec((1, 1, SUBLANES, nheads * CHUNK), lambda i, c: (i, c, 0, 0)),
                  pl.BlockSpec((SUBLANES, nheads * CHUNK), lambda i, c: (0, 0)),
                  pl.BlockSpec((1, nheads, DK_B, DV_B), lambda i, c: (i, 0, 0, 0)),
                  pl.BlockSpec((1, CHUNK, dm), lambda i, c: (i, c, 0)),
                  pl.BlockSpec((1, CHUNK, dm), lambda i, c: (i, c, 0)),
                  pl.BlockSpec((1, DV_B), lambda i, c: (0, 0))],
        out_specs=(pl.BlockSpec((1, CHUNK, dm), lambda i, c: (i, c, 0)),
                   pl.BlockSpec((1, nheads, DK_B, DV_B), lambda i, c: (i, 0, 0, 0))),
        scratch_shapes=[pltpu.VMEM((SUBLANES, cdim), F32)],
        compiler_params=_cparams(("arbitrary", "arbitrary"), 40),
        name="gated_delta",
    )(xqkv, st8, cw, ab, hp, s0, z, gb, gdn)


_T5_THRESHOLDS = tuple(math.ceil(8 * 2 ** (k / 2)) for k in range(1, 8))
_FAR_BUCKET = NUM_BUCKETS // 2 - 1


MASKED = -1e30
TOPK_CANDIDATES = 12


def _attn_kernel(itab, jtab, rb_ref, q_ref, qi_ref, sm_ref, kit_ref, ka_ref, kb_ref, va_ref, vb_ref, o_ref,
                 skey, cand, thr, jcut, qis, wib, madd, m_scr, l_scr, acc, dtile,
                 *, past, qb, tk, rb, ncand, n_sel, nheads):
    b = pl.program_id(0)
    step = pl.program_id(1)
    i = itab[step]
    j = jtab[step]
    hd = HD_A
    rep = tk // LANES
    kd = past // tk + i * (qb // tk)
    q0 = past + i * qb
    blk_b = kd - 2 * j
    blk_a = blk_b - 1

    def lanes(v):
        return jnp.concatenate([v] * rep, axis=1) if rep > 1 else v

    @pl.when((b == 0) & (step == 0))
    def _():
        rr = lax.broadcasted_iota(I32, (qb, 2 * tk), 0)
        cc = lax.broadcasted_iota(I32, (qb, 2 * tk), 1)
        rel = cc - tk - rr
        n = jnp.abs(rel)
        large = jnp.full((qb, 2 * tk), NUM_BUCKETS // 4, I32)
        for th in _T5_THRESHOLDS:
            large = large + jnp.where(n >= th, 1, 0)
        bucket = jnp.where(rel > 0, NUM_BUCKETS // 2, 0) + jnp.where(n < NUM_BUCKETS // 4, n, large)

        def head_body(h, _):
            base = rb_ref[_FAR_BUCKET, h]
            tile = jnp.zeros((qb, 2 * tk), F32)
            for bk in range(NUM_BUCKETS):
                if bk != _FAR_BUCKET:
                    tile = jnp.where(bucket == bk, (rb_ref[bk, h] - base) * LOG2E, tile)
            dtile[h] = tile
            return 0

        lax.fori_loop(0, nheads, head_body, 0)

    @pl.when(j == 0)
    def _():
        sm = sm_ref[0]
        for h in range(H_I):
            qis[h] = qi_ref[0, :, h * D_IDX:(h + 1) * D_IDX]
            wcol = sm[:, D_IDX + h:D_IDX + h + 1] * ((H_I * D_IDX) ** -0.5)
            wib[h] = jnp.broadcast_to(wcol, (qb, LANES))
        m_scr[...] = jnp.full(m_scr.shape, MASKED, F32)
        l_scr[...] = jnp.zeros(l_scr.shape, F32)
        acc[...] = jnp.zeros(acc.shape, F32)

        qpos = q0 + lax.broadcasted_iota(I32, (qb, tk), 0)
        lim = (qpos // CHUNK + 1) * CHUNK
        lane = lax.broadcasted_iota(I32, (qb, tk), 1)

        def score_tile(j, _):
            off = pl.multiple_of(j * tk, tk)
            kit = kit_ref[0, :, pl.ds(off, tk)]
            sc = jnp.zeros((qb, tk), F32)
            for h in range(H_I):
                d = jnp.dot(qis[h], kit, preferred_element_type=F32)
                sc = sc + jnp.maximum(d, 0.0) * lanes(wib[h])
            bits = pltpu.bitcast(sc + 0.0, I32)
            key = bits ^ ((bits >> 31) & 0x7FFFFFFF)
            key = jnp.where(lane + off < lim, key, INT_MIN)
            skey[:, pl.ds(off, tk)] = key
            for u in range(rep):
                x = key[:, u * LANES:(u + 1) * LANES]
                for c in range(ncand):
                    a = cand[c]
                    cand[c] = jnp.maximum(a, x)
                    x = jnp.minimum(a, x)
            return 0

        cand[...] = jnp.full(cand.shape, INT_MIN, I32)
        lax.fori_loop(0, kd + 1, score_tile, 0)

        def kth_largest(count_ge, rows):
            def bit_body(bi, t):
                cnd = t + lax.shift_left(jnp.int32(1), jnp.int32(31) - bi)
                return jnp.where(count_ge(cnd) >= n_sel, cnd, t)
            return lax.fori_loop(0, 32, bit_body, jnp.full((rows, LANES), INT_MIN, I32))

        def row_total(c):
            return jnp.broadcast_to(jnp.sum(c, axis=1, keepdims=True), c.shape)

        def cand_count_ge(cnd):
            c = jnp.zeros((qb, LANES), I32)
            for k in range(ncand):
                c = c + jnp.where(cand[k] >= cnd, 1, 0)
            return row_total(c)

        thr[...] = kth_largest(cand_count_ge, qb)

        def finish_rows(rblk, bad, full_search):
            r0 = pl.multiple_of(rblk * rb, rb)
            rlane = lax.broadcasted_iota(I32, (rb, tk), 1)

            def count(pred):
                def body(jj, c):
                    off = pl.multiple_of(jj * tk, tk)
                    mk = jnp.where(pred(skey[pl.ds(r0, rb), pl.ds(off, tk)], off), 1, 0)
                    part = mk[:, :LANES]
                    for u in range(1, rep):
                        part = part + mk[:, u * LANES:(u + 1) * LANES]
                    return c + part
                return row_total(lax.fori_loop(0, kd + 1, body, jnp.zeros((rb, LANES), I32)))

            if full_search:
                t = kth_largest(lambda cnd: count(lambda x, off: x >= lanes(cnd)), rb)
            else:
                t = thr[pl.ds(r0, rb), :]
            ts = jnp.maximum(t, INT_MIN + 1)
            thr[pl.ds(r0, rb), :] = ts
            jcut[pl.ds(r0, rb), :] = jnp.full((rb, LANES), 2 ** 30, I32)

            tsl = lanes(ts)
            c_gt = count(lambda x, off: x > tsl)
            c_eq = count(lambda x, off: x == tsl)
            need = n_sel - c_gt
            excess = jnp.max(jnp.where((c_eq > need) & (t != INT_MIN), 1.0, 0.0))

            @pl.when(excess > 0.5)
            def _():
                def jbit(bi, jc):
                    cnd = jc | lax.shift_left(jnp.int32(1), jnp.int32(15) - bi)
                    cl = lanes(cnd)
                    cnt = count(lambda x, off: (x == tsl) & (rlane + off < cl))
                    return jnp.where(cnt <= need, cnd, jc)
                jcut[pl.ds(r0, rb), :] = lax.fori_loop(0, 16, jbit, jnp.zeros((rb, LANES), I32))

            return jnp.maximum(bad, jnp.max(jnp.where(need <= 0, 1.0, 0.0)))

        bad = lax.fori_loop(0, qb // rb, functools.partial(finish_rows, full_search=False), jnp.float32(0.0))

        @pl.when(bad > 0.5)
        def _():
            lax.fori_loop(0, qb // rb, functools.partial(finish_rows, full_search=True), jnp.float32(0.0))

    offa = pl.multiple_of(jnp.maximum(blk_a, 0) * tk, tk)
    offb = pl.multiple_of(blk_b * tk, tk)
    ts = lanes(thr[...])
    jc = lanes(jcut[...])
    lane = lax.broadcasted_iota(I32, (qb, tk), 1)
    hang = jnp.where(blk_a < 0, MASKED, 0.0)
    for off, col0, extra in ((offa, 0, hang), (offb, tk, 0.0)):
        x = skey[:, pl.ds(off, tk)]
        sel = (x > ts) | ((x == ts) & (lane + off < jc))
        madd[:, col0:col0 + tk] = jnp.where(sel, 0.0, MASKED) + extra

    def flash(near):
        for h in range(nheads):
            hs = slice(h * hd, (h + 1) * hd)
            qh = q_ref[0, :, hs]
            s = jnp.concatenate(
                [lax.dot_general(qh, ka_ref[0, :, hs], _NT, preferred_element_type=F32),
                 lax.dot_general(qh, kb_ref[0, :, hs], _NT, preferred_element_type=F32)], axis=1)
            s = s + madd[...]
            if near:
                s = s + dtile[h]
            m_prev = m_scr[h]
            m_new = jnp.maximum(m_prev, jnp.max(s, axis=1, keepdims=True))
            alpha = jnp.exp2(m_prev - m_new)
            p = jnp.exp2(s - jnp.concatenate([m_new] * (2 * rep), axis=1))
            l_scr[h] = alpha * l_scr[h] + jnp.sum(p, axis=1, keepdims=True)
            m_scr[h] = m_new
            pb = p.astype(BF16)
            pv = (jnp.dot(pb[:, :tk], va_ref[0, :, hs], preferred_element_type=F32)
                  + jnp.dot(pb[:, tk:], vb_ref[0, :, hs], preferred_element_type=F32))
            acc[:, hs] = alpha * acc[:, hs] + pv

    @pl.when(j == 0)
    def _():
        flash(True)

    @pl.when(j > 0)
    def _():
        flash(False)

    @pl.when(j == kd // 2)
    def _():
        for h in range(nheads):
            hs = slice(h * hd, (h + 1) * hd)
            o_ref[0, :, hs] = (acc[:, hs] / l_scr[h]).astype(o_ref.dtype)


def _attention(rel_bias, q, qi, small, kit, kall, vall, *, past, n_sel):
    b, t, dm = q.shape
    lp = kall.shape[1]
    nheads = dm // HD_A
    qb = min(t, 256)
    tk = 256
    nq = t // qb
    assert past % tk == 0 and lp % tk == 0 and t % qb == 0
    assert qb == tk or nq == 1
    assert lp < 2 ** 16
    kdiag = lambda i: past // tk + i * (qb // tk)
    steps = [(i, j) for i in range(nq) for j in range(kdiag(i) // 2 + 1)]
    itab = jnp.asarray([s[0] for s in steps], I32)
    jtab = jnp.asarray([s[1] for s in steps], I32)
    kern = functools.partial(_attn_kernel, past=past, qb=qb, tk=tk, rb=min(qb, 128), ncand=TOPK_CANDIDATES,
                             n_sel=n_sel, nheads=nheads)

    def row(bi, s, it, jt):
        return (bi, it[s], 0)

    def tile_a(bi, s, it, jt):
        return (bi, jnp.maximum(kdiag(it[s]) - 2 * jt[s] - 1, 0), 0)

    def tile_b(bi, s, it, jt):
        return (bi, kdiag(it[s]) - 2 * jt[s], 0)

    grid_spec = pltpu.PrefetchScalarGridSpec(
        num_scalar_prefetch=2,
        grid=(b, len(steps)),
        in_specs=[pl.BlockSpec(memory_space=pltpu.SMEM),
                  pl.BlockSpec((1, qb, dm), row),
                  pl.BlockSpec((1, qb, H_I * D_IDX), row),
                  pl.BlockSpec((1, qb, LANES), row),
                  pl.BlockSpec((1, D_IDX, lp), lambda bi, s, it, jt: (bi, 0, 0), pipeline_mode=pl.Buffered(1)),
                  pl.BlockSpec((1, tk, dm), tile_a),
                  pl.BlockSpec((1, tk, dm), tile_b),
                  pl.BlockSpec((1, tk, dm), tile_a),
                  pl.BlockSpec((1, tk, dm), tile_b)],
        out_specs=pl.BlockSpec((1, qb, dm), row),
        scratch_shapes=[pltpu.VMEM((qb, lp), I32),
                        pltpu.VMEM((TOPK_CANDIDATES, qb, LANES), I32),
                        pltpu.VMEM((qb, LANES), I32),
                        pltpu.VMEM((qb, LANES), I32),
                        pltpu.VMEM((H_I, qb, D_IDX), BF16),
                        pltpu.VMEM((H_I, qb, LANES), F32),
                        pltpu.VMEM((qb, 2 * tk), F32),
                        pltpu.VMEM((nheads, qb, LANES), F32),
                        pltpu.VMEM((nheads, qb, LANES), F32),
                        pltpu.VMEM((qb, dm), F32),
                        pltpu.VMEM((nheads, qb, 2 * tk), F32)])
    return pl.pallas_call(
        kern,
        out_shape=jax.ShapeDtypeStruct((b, t, dm), BF16),
        grid_spec=grid_spec,
        compiler_params=_cparams(("arbitrary", "arbitrary"), 60),
        name="sparse_attn",
    )(itab, jtab, rel_bias, q, qi, small, kit, kall, kall, vall, vall)


def _pad_rows(a, rows):
    return jnp.pad(a, [(0, 0), (rows - a.shape[1], 0)] + [(0, 0)] * (a.ndim - 2))


def _run_group(x, mod, cache_k, cache_v, cache_kidx, st_conv, st_delta, st_ffn, rel_bias, g_final, w):
    b, t, d = x.shape
    nheads = d // HD_A
    sh1, sc1, gt1, sh2, sc2, gt2 = [m[:, None, :] for m in jnp.split(mod, 6, axis=-1)]
    m_rows = b * t

    h = _norm_mod(x, w["g_norm1"], sc1, sh1).reshape(m_rows, d)
    proj = lambda name, dt=F32: _matmul(h, w[name], dt, name="proj_" + name).reshape(b, t, -1)
    qa, qi = proj("w_qa", BF16), proj("w_qi", BF16)
    ka, va = proj("w_ka"), proj("w_va")
    small = proj("w_small")
    qkv_b, z_b, gate_a, gate_b = proj("w_qkvb"), proj("w_z"), proj("w_ga"), proj("w_gb")
    ki = small[..., :D_IDX]

    past = 0 if cache_k is None else cache_k.shape[1]
    l_all = past + t
    tk = 256
    lp = -(-l_all // tk) * tk
    if cache_k is None:
        k_all, v_all, ki_all = ka, va, ki
    else:
        k_all = jnp.concatenate([cache_k.reshape(b, past, d), ka], axis=1)
        v_all = jnp.concatenate([cache_v.reshape(b, past, d), va], axis=1)
        ki_all = jnp.concatenate([cache_kidx, ki], axis=1)
    padk = lambda a: jnp.pad(a, ((0, 0), (0, lp - l_all), (0, 0))).astype(BF16)
    k_all, v_all = padk(k_all), padk(v_all)
    kit = jnp.swapaxes(padk(ki_all), 1, 2)
    n_sel = min(TOPK_MAX, l_all // 4)
    o_a = _attention(rel_bias, qa, qi, small, kit, k_all, v_all, past=past, n_sel=n_sel)

    n = t // CHUNK
    to_rows = lambda v: jnp.swapaxes(v.reshape(b, n, CHUNK, nheads), 2, 3).reshape(b, n, 1, nheads * CHUNK)
    ab = jnp.concatenate([to_rows(small[..., 80:96]), to_rows(small[..., 96:112]),
                          jnp.zeros((b, n, SUBLANES - 2, nheads * CHUNK), F32)], axis=2)
    ogb, new_delta = _delta(qkv_b, _pad_rows(st_conv, SUBLANES), w["conv_b_w"], ab, w["hp_rows"],
                            st_delta, z_b, gate_b, w["g_delta_norm"])
    assert t >= CONVB_WIDTH - 1
    new_conv = qkv_b[:, t - (CONVB_WIDTH - 1):]

    x1 = _merge_out(o_a, gate_a, ogb, w["w_out"], x, gt1)
    h2 = _norm_mod(x1, w["g_norm2"], sc2, sh2)
    y, tail = _ffn(h2, w["w_ffn_gate"], w["w_ffn_up"], w["w_ffn_down"], w["ffn_conv_w"], w["ffn_conv_b"],
                   st_ffn, x1, gt2, g_final)
    new_ffn = tail[:, -1, SUBLANES - (FFN_CONV_WIDTH - 1):]
    kshape = (1, b, t, nheads, HD_A)
    return (y, ka.reshape(kshape), va.reshape(kshape), ki[None], new_conv[None], new_delta[None], new_ffn[None])


def kernel(x_prompt, x_sample, c_prompt, c_sample, cache_k, cache_v, cache_kidx, state_conv_b, state_delta,
           state_ffn_conv, rel_bias, g_final, w_ada, b_ada, g_norm1, w_in, conv_b_w, a_log, dt_bias,
           g_delta_norm, w_out, g_norm2, w_ffn_gate, ffn_conv_w, ffn_conv_b, w_ffn_up, w_ffn_down):
    assert w_ada.shape[0] == 1, "one layer"
    d = x_prompt.shape[-1]
    nheads = d // HD_A
    bp, bs = x_prompt.shape[0], x_sample.shape[0]
    convb = 3 * nheads * DK_B

    wi_ = w_in[0]
    sizes = (d, d, d, H_I * D_IDX, D_IDX, H_I, convb, nheads, nheads, d, d, d)
    offs = [0]
    for s in sizes:
        offs.append(offs[-1] + s)
    col = lambda k: wi_[:, offs[k]:offs[k + 1]]
    w_small = jnp.concatenate([col(4), col(5), col(7), col(8),
                               jnp.zeros((d, LANES - D_IDX - H_I - 2 * nheads), F32)], axis=1)
    bf = lambda a: a.astype(BF16)
    w = dict(
        w_qa=bf(col(0) * (HD_A ** -0.5 * LOG2E)), w_ka=bf(col(1)), w_va=bf(col(2)), w_qi=bf(col(3)), w_small=bf(w_small),
        w_qkvb=bf(col(6)), w_z=bf(col(9)), w_ga=bf(col(10)), w_gb=bf(col(11)),
        g_norm1=g_norm1, g_norm2=g_norm2, conv_b_w=conv_b_w[0], g_delta_norm=g_delta_norm,
        w_out=bf(w_out[0]), w_ffn_gate=bf(w_ffn_gate[0]), w_ffn_up=bf(w_ffn_up[0]),
        w_ffn_down=bf(w_ffn_down[0]), ffn_conv_w=ffn_conv_w[0], ffn_conv_b=ffn_conv_b,
        hp_rows=jnp.concatenate([jnp.repeat(a_log[0], CHUNK)[None], jnp.repeat(dt_bias[0], CHUNK)[None],
                                 jnp.zeros((SUBLANES - 2, nheads * CHUNK), F32)], axis=0),
    )

    c_all = jnp.concatenate([c_prompt, c_sample], axis=0)
    rows = -(-c_all.shape[0] // SUBLANES) * SUBLANES
    c_all = jnp.pad(c_all, ((0, rows - c_all.shape[0]), (0, 0)))
    mod = _adaln(c_all, w_ada[0], b_ada)

    zeros = lambda *s: jnp.zeros(s, F32)
    outs_p = _run_group(x_prompt, mod[:bp], None, None, None,
                        zeros(bp, CONVB_WIDTH - 1, convb), zeros(bp, nheads, DK_B, DV_B),
                        zeros(bp, FFN_CONV_WIDTH - 1, w_ffn_gate.shape[-1]), rel_bias, g_final[None], w)
    outs_s = _run_group(x_sample, mod[bp:bp + bs], cache_k[0], cache_v[0], cache_kidx[0],
                        state_conv_b[0], state_delta[0], state_ffn_conv[0], rel_bias, g_final[None], w)
    return (outs_p[0], outs_s[0]) + tuple(outs_p[1:]) + tuple(outs_s[1:])
```

```python
import functools
import math

import jax
import jax.numpy as jnp
from jax import lax
from jax.experimental import pallas as pl
from jax.experimental.pallas import tpu as pltpu

F32 = jnp.float32
BF16 = jnp.bfloat16
I32 = jnp.int32

CHUNK = 64
HD_A = 128
H_I = 16
D_IDX = 64
TOPK_MAX = 256
NUM_BUCKETS = 32
MAX_DISTANCE = 128
DK_B = 128
DV_B = 128
CONVB_WIDTH = 4
FFN_CONV_WIDTH = 3
EPS = 1e-6
LOG2E = math.log2(math.e)

V7X_VMEM_BYTES = 64 * 1024 * 1024
LANES = 128
SUBLANES = 8
INT_MIN = -(2 ** 31)


def _cparams(sem, vmem_mb):
    assert vmem_mb * 1024 * 1024 < V7X_VMEM_BYTES
    return pltpu.CompilerParams(dimension_semantics=sem, vmem_limit_bytes=vmem_mb * 1024 * 1024)


def _sigmoid(x):
    return 1.0 / (1.0 + jnp.exp(-x))


def _silu(x):
    return x * _sigmoid(x)


def _dot(a, b, dims=None, passes=1):
    dn = dims if dims is not None else (((a.ndim - 1,), (0,)), ((), ()))
    if passes == 6:
        return lax.dot_general(a, b, dn, precision=lax.Precision.HIGHEST, preferred_element_type=F32)
    ah = a.astype(BF16)
    bh = b.astype(BF16)
    out = lax.dot_general(ah, bh, dn, preferred_element_type=F32)
    if passes == 3:
        al = (a - ah.astype(F32)).astype(BF16)
        bl = (b - bh.astype(F32)).astype(BF16)
        out = out + lax.dot_general(ah, bl, dn, preferred_element_type=F32)
        out = out + lax.dot_general(al, bh, dn, preferred_element_type=F32)
    return out


_NT = (((1,), (1,)), ((), ()))
_TN = (((0,), (0,)), ((), ()))


def _adaln_kernel(c_ref, w_ref, b_ref, o_ref):
    o_ref[...] = _dot(_silu(c_ref[...]), w_ref[...]) + b_ref[...]


def _adaln(c, w, b):
    bp, d = c.shape
    n = w.shape[1]
    tn = 1024
    return pl.pallas_call(
        _adaln_kernel,
        out_shape=jax.ShapeDtypeStruct((bp, n), F32),
        grid=(n // tn,),
        in_specs=[pl.BlockSpec((bp, d), lambda j: (0, 0)),
                  pl.BlockSpec((d, tn), lambda j: (0, j)),
                  pl.BlockSpec((1, tn), lambda j: (0, j))],
        out_specs=pl.BlockSpec((bp, tn), lambda j: (0, j)),
        compiler_params=_cparams(("arbitrary",), 40),
        name="adaln",
    )(c, w, b)


def _norm_mod_kernel(x_ref, g_ref, sc_ref, sh_ref, o_ref):
    x = x_ref[...]
    y = x * lax.rsqrt(jnp.mean(x * x, axis=-1, keepdims=True) + EPS)
    o_ref[...] = ((y * g_ref[...]) * (1.0 + sc_ref[...]) + sh_ref[...]).astype(o_ref.dtype)


def _row_tiles(b, t, rows):
    if t >= rows:
        return 1, rows
    bb = max(1, min(b, rows // t))
    while b % bb:
        bb -= 1
    return bb, t


def _norm_mod(x, g, sc, sh):
    b, t, d = x.shape
    bb, tt = _row_tiles(b, t, 512)
    return pl.pallas_call(
        _norm_mod_kernel,
        out_shape=jax.ShapeDtypeStruct((b, t, d), BF16),
        grid=(b // bb, t // tt),
        in_specs=[pl.BlockSpec((bb, tt, d), lambda i, j: (i, j, 0)),
                  pl.BlockSpec((1, d), lambda i, j: (0, 0)),
                  pl.BlockSpec((bb, 1, d), lambda i, j: (i, 0, 0)),
                  pl.BlockSpec((bb, 1, d), lambda i, j: (i, 0, 0))],
        out_specs=pl.BlockSpec((bb, tt, d), lambda i, j: (i, j, 0)),
        compiler_params=_cparams(("arbitrary", "arbitrary"), 32),
        name="norm_mod",
    )(x, g, sc, sh)


def _mm_kernel(a_ref, w_ref, o_ref):
    o_ref[...] = jnp.dot(a_ref[...], w_ref[...], preferred_element_type=F32).astype(o_ref.dtype)


def _matmul(a, w, out_dtype=F32, name="proj"):
    m, k = a.shape
    n = w.shape[1]
    tm = min(m, 1024)
    tn = min(n, 512)
    return pl.pallas_call(
        _mm_kernel,
        out_shape=jax.ShapeDtypeStruct((m, n), out_dtype),
        grid=(m // tm, n // tn),
        in_specs=[pl.BlockSpec((tm, k), lambda i, j: (i, 0)),
                  pl.BlockSpec((k, tn), lambda i, j: (0, j))],
        out_specs=pl.BlockSpec((tm, tn), lambda i, j: (i, j)),
        compiler_params=_cparams(("arbitrary", "arbitrary"), 40),
        name=name,
    )(a, w)


def _merge_out_kernel(oa_ref, ga_ref, ogb_ref, w_ref, x_ref, gt_ref, o_ref, mixed):
    bb, tt, k = oa_ref.shape

    @pl.when(pl.program_id(2) == 0)
    def _():
        mix = _sigmoid(ga_ref[...]) * oa_ref[...] + ogb_ref[...]
        mixed[...] = mix.reshape(bb * tt, k).astype(BF16)

    y = jnp.dot(mixed[...], w_ref[...], preferred_element_type=F32)
    o_ref[...] = x_ref[...] + gt_ref[...] * y.reshape(bb, tt, -1)


def _merge_out(oa, ga, ogb, w, x, gt):
    b, t, k = oa.shape
    n = w.shape[1]
    bb, tt = _row_tiles(b, t, 256)
    tn = 512
    row = lambda i, s, j: (i, s, 0)
    return pl.pallas_call(
        _merge_out_kernel,
        out_shape=jax.ShapeDtypeStruct((b, t, n), F32),
        grid=(b // bb, t // tt, n // tn),
        in_specs=[pl.BlockSpec((bb, tt, k), row),
                  pl.BlockSpec((bb, tt, k), row),
                  pl.BlockSpec((bb, tt, k), row),
                  pl.BlockSpec((k, tn), lambda i, s, j: (0, j)),
                  pl.BlockSpec((bb, tt, tn), lambda i, s, j: (i, s, j)),
                  pl.BlockSpec((bb, 1, tn), lambda i, s, j: (i, 0, j))],
        out_specs=pl.BlockSpec((bb, tt, tn), lambda i, s, j: (i, s, j)),
        scratch_shapes=[pltpu.VMEM((bb * tt, k), BF16)],
        compiler_params=_cparams(("arbitrary", "arbitrary", "arbitrary"), 40),
        name="merge_out",
    )(oa, ga, ogb, w, x, gt)


def _ffn_kernel(h_ref, wg_ref, wu_ref, wd_ref, cw_ref, cb_ref, st_ref, x_ref, gt_ref, gf_ref,
                y_ref, tail_ref, acc, carry):
    s = pl.program_id(1)
    f = pl.program_id(2)
    nf = pl.num_programs(2)
    bb, tt, d = h_ref.shape
    tm = bb * tt
    tf = wg_ref.shape[1]

    h = h_ref[...].reshape(tm, d)
    gate = jnp.dot(h, wg_ref[...], preferred_element_type=F32)
    up = jnp.dot(h, wu_ref[...], preferred_element_type=F32)

    st = st_ref[...]
    first = s == 0

    @pl.when(first)
    def _():
        carry[f] = jnp.zeros((SUBLANES, tf), F32)

    prev = carry[f]
    p0 = jnp.where(first, st[:, 0:1, :], prev[6:7, :][None])
    p1 = jnp.where(first, st[:, 1:2, :], prev[7:8, :][None])
    p0 = jnp.broadcast_to(p0, (bb, tt, tf)).reshape(tm, tf)
    p1 = jnp.broadcast_to(p1, (bb, tt, tf)).reshape(tm, tf)
    tpos = lax.broadcasted_iota(I32, (tm, tf), 0) % tt
    r1 = pltpu.roll(gate, 1, axis=0)
    r2 = pltpu.roll(gate, 2, axis=0)
    x1 = jnp.where(tpos == 0, p1, r1)
    x2 = jnp.where(tpos == 0, p0, jnp.where(tpos == 1, p1, r2))
    cw = cw_ref[...]
    a = cw[0:1] * x2 + cw[1:2] * x1 + cw[2:3] * gate + cb_ref[...]
    act = (_silu(a) * up).astype(BF16)
    contrib = jnp.dot(act, wd_ref[...], preferred_element_type=F32)

    g3 = gate.reshape(bb, tt, tf)
    tail_ref[...] = g3[:, None, tt - SUBLANES:, :]
    carry[f] = gate[tm - SUBLANES:, :]

    @pl.when(f == 0)
    def _():
        acc[...] = contrib

    @pl.when(f > 0)
    def _():
        acc[...] += contrib

    @pl.when(f == nf - 1)
    def _():
        x = x_ref[...] + gt_ref[...] * acc[...].reshape(bb, tt, d)
        y = x * lax.rsqrt(jnp.mean(x * x, axis=-1, keepdims=True) + EPS)
        y_ref[...] = y * gf_ref[...]


def _ffn(h2, wg, wu, wd, cw, cb, st, x1, gt2, gf):
    b, t, d = h2.shape
    dff = wg.shape[1]
    bb, tt = _row_tiles(b, t, 512)
    tf = 512
    nf = dff // tf
    row = lambda i, s, f: (i, s, 0)
    return pl.pallas_call(
        _ffn_kernel,
        out_shape=(jax.ShapeDtypeStruct((b, t, d), F32),
                   jax.ShapeDtypeStruct((b, t // tt, SUBLANES, dff), F32)),
        grid=(b // bb, t // tt, nf),
        in_specs=[pl.BlockSpec((bb, tt, d), row),
                  pl.BlockSpec((d, tf), lambda i, s, f: (0, f)),
                  pl.BlockSpec((d, tf), lambda i, s, f: (0, f)),
                  pl.BlockSpec((tf, d), lambda i, s, f: (f, 0)),
                  pl.BlockSpec((FFN_CONV_WIDTH, tf), lambda i, s, f: (0, f)),
                  pl.BlockSpec((1, tf), lambda i, s, f: (0, f)),
                  pl.BlockSpec((bb, FFN_CONV_WIDTH - 1, tf), lambda i, s, f: (i, 0, f)),
                  pl.BlockSpec((bb, tt, d), row),
                  pl.BlockSpec((bb, 1, d), lambda i, s, f: (i, 0, 0)),
                  pl.BlockSpec((1, d), lambda i, s, f: (0, 0))],
        out_specs=(pl.BlockSpec((bb, tt, d), row),
                   pl.BlockSpec((bb, 1, SUBLANES, tf), lambda i, s, f: (i, s, 0, f))),
        scratch_shapes=[pltpu.VMEM((bb * tt, d), F32),
                        pltpu.VMEM((nf, SUBLANES, tf), F32)],
        compiler_params=_cparams(("arbitrary", "arbitrary", "arbitrary"), 48),
        name="conv_ffn",
    )(h2, wg, wu, wd, cw, cb, st, x1, gt2, gf)


HEADS_PER_GROUP = 4
GROUP_ROWS = HEADS_PER_GROUP * CHUNK
_DELTA_PASSES = dict(kk=1, inv=1, uw=1, qk=1, state=1, intra=1, upd=1)


def _delta_kernel(x_ref, st_ref, cw_ref, ab_ref, hp_ref, s0_ref, z_ref, gb_ref, gdn_ref,
                  o_ref, s_ref, carry, *, nheads):
    c = pl.program_id(1)
    r = GROUP_ROWS
    hd = DK_B

    @pl.when(c == 0)
    def _():
        carry[...] = st_ref[0]
        s_ref[...] = s0_ref[...]

    prev = carry[...]
    carry[...] = x_ref[0, CHUNK - SUBLANES:, :]

    ri = lax.broadcasted_iota(I32, (r, r), 0)
    ci = lax.broadcasted_iota(I32, (r, r), 1)
    same_head = (ri // CHUNK) == (ci // CHUNK)
    incl = same_head & (ri >= ci)
    strict = same_head & (ri > ci)
    triu = jnp.where(same_head & (ri <= ci), 1.0, 0.0).astype(F32)
    eye = jnp.where(ri == ci, 1.0, 0.0).astype(F32)
    row8 = lax.broadcasted_iota(I32, (SUBLANES, r * 2), 0)
    rid = lax.broadcasted_iota(I32, (LANES, r), 0)

    def conv_silu(col0, width):
        xs = x_ref[0, :, col0:col0 + width]
        pv = prev[:, col0:col0 + width]
        w = cw_ref[:, col0:col0 + width]
        y = xs * w[CONVB_WIDTH - 1:CONVB_WIDTH]
        for k in range(1, CONVB_WIDTH):
            rolled = pltpu.roll(xs, k, axis=0)
            head = jnp.where(row8[:, :width] < k, pltpu.roll(pv, k, axis=0), rolled[:SUBLANES])
            shifted = jnp.concatenate([head, rolled[SUBLANES:]], axis=0)
            y = y + shifted * w[CONVB_WIDTH - 1 - k:CONVB_WIDTH - k]
        return _silu(y)

    def l2n(v):
        return v * lax.rsqrt(jnp.sum(v * v, axis=-1, keepdims=True) + EPS)

    def stack(y, fn):
        return jnp.concatenate([fn(y[:, j * hd:(j + 1) * hd]) for j in range(HEADS_PER_GROUP)], axis=0)

    ngroups = nheads // HEADS_PER_GROUP
    gw = HEADS_PER_GROUP * hd
    pp = _DELTA_PASSES

    def prep(g):
        qs = stack(conv_silu(g * gw, gw), lambda v: l2n(v) * (DK_B ** -0.5))
        ks = stack(conv_silu(nheads * hd + g * gw, gw), l2n)
        vs = stack(conv_silu(2 * nheads * hd + g * gw, gw), lambda v: v)
        a_row = ab_ref[0, 0, 0:1, g * r:(g + 1) * r]
        b_row = ab_ref[0, 0, 1:2, g * r:(g + 1) * r]
        alog = hp_ref[0:1, g * r:(g + 1) * r]
        dtb = hp_ref[1:2, g * r:(g + 1) * r]
        xsp = a_row + dtb
        softplus = jnp.maximum(xsp, 0.0) + jnp.log1p(jnp.exp(-jnp.abs(xsp)))
        g_row = -jnp.exp(alog) * softplus
        beta_row = _sigmoid(b_row)
        gc_row = _dot(jnp.broadcast_to(g_row, (SUBLANES, r)), triu, passes=6)[0:1]
        rows = jnp.where(rid == 0, beta_row, jnp.where(rid == 1, gc_row, 0.0))
        cols = rows.T
        beta_col = cols[:, 0:1]
        gc_col = cols[:, 1:2]
        gl_col = jnp.concatenate(
            [jnp.broadcast_to(cols[(j + 1) * CHUNK - 1:(j + 1) * CHUNK, 1:2], (CHUNK, 1))
             for j in range(HEADS_PER_GROUP)], axis=0)
        decay = jnp.where(incl, jnp.exp(gc_col - gc_row), 0.0)
        egc = jnp.exp(gc_col)
        kb = ks * beta_col
        return dict(qs=qs, ks=ks, kb=kb, rhs=jnp.concatenate([vs * beta_col, kb * egc], axis=1),
                    decay=decay, qg=qs * egc, kd=ks * jnp.exp(gl_col - gc_col), gl_col=gl_col)

    gs = [prep(g) for g in range(ngroups)]
    ms = [jnp.where(strict, _dot(d["kb"], d["ks"], _NT, passes=pp["kk"]) * d["decay"], 0.0) for d in gs]

    invs = [eye - jnp.where((ri // 2 == ci // 2), m, 0.0) for m in ms]
    bs = 2
    while bs < CHUNK:
        lower = ((ri // (2 * bs)) == (ci // (2 * bs))) & ((ri % (2 * bs)) >= bs) & ((ci % (2 * bs)) < bs)
        xc = [_dot(inv, jnp.where(lower, m, 0.0), passes=pp["inv"]) for inv, m in zip(invs, ms)]
        invs = [inv - _dot(t, inv, passes=pp["inv"]) for inv, t in zip(invs, xc)]
        bs *= 2

    uws = [_dot(inv, d["rhs"], passes=pp["uw"]) for inv, d in zip(invs, gs)]
    attns = [jnp.where(incl, _dot(d["qs"], d["ks"], _NT, passes=pp["qk"]) * d["decay"], 0.0) for d in gs]

    vnews, qstates = [], []
    for g in range(ngroups):
        vnew, qstate = [], []
        for j in range(HEADS_PER_GROUP):
            sl = slice(j * CHUNK, (j + 1) * CHUNK)
            wq = jnp.concatenate([uws[g][sl, hd:], gs[g]["qg"][sl]], axis=0)
            rs = _dot(wq, s_ref[0, g * HEADS_PER_GROUP + j], passes=pp["state"])
            vnew.append(uws[g][sl, :hd] - rs[:CHUNK])
            qstate.append(rs[CHUNK:])
        vnews.append(vnew)
        qstates.append(qstate)
    intras = [_dot(attns[g], jnp.concatenate(vnews[g], axis=0), passes=pp["intra"]) for g in range(ngroups)]

    for g in range(ngroups):
        for j in range(HEADS_PER_GROUP):
            hidx = g * HEADS_PER_GROUP + j
            sl = slice(j * CHUNK, (j + 1) * CHUNK)
            hs = slice(hidx * hd, (hidx + 1) * hd)
            o = qstates[g][j] + intras[g][sl]
            glast = jnp.exp(gs[g]["gl_col"][j * CHUNK:j * CHUNK + 1, :])
            upd = _dot(gs[g]["kd"][sl], vnews[g][j], _TN, passes=pp["upd"])
            s_ref[0, hidx] = s_ref[0, hidx] * glast + upd
            on = o * lax.rsqrt(jnp.mean(o * o, axis=-1, keepdims=True) + EPS) * gdn_ref[...]
            o_ref[0, :, hs] = on * _silu(z_ref[0, :, hs]) * _sigmoid(gb_ref[0, :, hs])


def _delta(xqkv, st8, cw, ab, hp, s0, z, gb, gdn):
    b, t, cdim = xqkv.shape
    nheads = s0.shape[1]
    dm = z.shape[2]
    n = t // CHUNK
    kern = functools.partial(_delta_kernel, nheads=nheads)
    return pl.pallas_call(
        kern,
        out_shape=(jax.ShapeDtypeStruct((b, t, dm), F32),
                   jax.ShapeDtypeStruct(s0.shape, F32)),
        grid=(b, n),
        in_specs=[pl.BlockSpec((1, CHUNK, cdim), lambda i, c: (i, c, 0)),
                  pl.BlockSpec((1, SUBLANES, cdim), lambda i, c: (i, 0, 0)),
                  pl.BlockSpec((CONVB_WIDTH, cdim), lambda i, c: (0, 0)),
                  pl.BlockSpec((1, 1, SUBLANES, nheads * CHUNK), lambda i, c: (i, c, 0, 0)),
                  pl.BlockSpec((SUBLANES, nheads * CHUNK), lambda i, c: (0, 0)),
                  pl.BlockSpec((1, nheads, DK_B, DV_B), lambda i, c: (i, 0, 0, 0)),
                  pl.BlockSpec((1, CHUNK, dm), lambda i, c: (i, c, 0)),
                  pl.BlockSpec((1, CHUNK, dm), lambda i, c: (i, c, 0)),
                  pl.BlockSpec((1, DV_B), lambda i, c: (0, 0))],
        out_specs=(pl.BlockSpec((1, CHUNK, dm), lambda i, c: (i, c, 0)),
                   pl.BlockSpec((1, nheads, DK_B, DV_B), lambda i, c: (i, 0, 0, 0))),
        scratch_shapes=[pltpu.VMEM((SUBLANES, cdim), F32)],
        compiler_params=_cparams(("arbitrary", "arbitrary"), 40),
        name="gated_delta",
    )(xqkv, st8, cw, ab, hp, s0, z, gb, gdn)


_T5_THRESHOLDS = tuple(math.ceil(8 * 2 ** (k / 2)) for k in range(1, 8))
_FAR_BUCKET = NUM_BUCKETS // 2 - 1


MASKED = -1e30
TOPK_CANDIDATES = 12
KEY_TILE = 256
FLASH_ROWS = 512


def _topk_kernel(qi_ref, sm_ref, kit_ref, mask_ref, skey, cand, thr, jcut, qis, wib,
                 *, past, qb, tk, rb, ncand, n_sel, nkt):
    i = pl.program_id(1)
    rep = tk // LANES
    kd = past // tk + i * (qb // tk)
    q0 = past + i * qb

    def lanes(v):
        return jnp.concatenate([v] * rep, axis=1) if rep > 1 else v

    if True:
        sm = sm_ref[0]
        for h in range(H_I):
            qis[h] = qi_ref[0, :, h * D_IDX:(h + 1) * D_IDX]
            wcol = sm[:, D_IDX + h:D_IDX + h + 1] * ((H_I * D_IDX) ** -0.5)
            wib[h] = jnp.broadcast_to(wcol, (qb, LANES))

        qpos = q0 + lax.broadcasted_iota(I32, (qb, tk), 0)
        lim = (qpos // CHUNK + 1) * CHUNK
        lane = lax.broadcasted_iota(I32, (qb, tk), 1)

        def score_tile(j, _):
            off = pl.multiple_of(j * tk, tk)
            kit = kit_ref[0, :, pl.ds(off, tk)]
            sc = jnp.zeros((qb, tk), F32)
            for h in range(H_I):
                d = jnp.dot(qis[h], kit, preferred_element_type=F32)
                sc = sc + jnp.maximum(d, 0.0) * lanes(wib[h])
            bits = pltpu.bitcast(sc + 0.0, I32)
            key = bits ^ ((bits >> 31) & 0x7FFFFFFF)
            key = jnp.where(lane + off < lim, key, INT_MIN)
            skey[:, pl.ds(off, tk)] = key
            for u in range(rep):
                x = key[:, u * LANES:(u + 1) * LANES]
                for c in range(ncand):
                    a = cand[c]
                    cand[c] = jnp.maximum(a, x)
                    x = jnp.minimum(a, x)
            return 0

        cand[...] = jnp.full(cand.shape, INT_MIN, I32)
        lax.fori_loop(0, kd + 1, score_tile, 0)

        def kth_largest(count_ge, rows):
            def bit_body(bi, t):
                cnd = t + lax.shift_left(jnp.int32(1), jnp.int32(31) - bi)
                return jnp.where(count_ge(cnd) >= n_sel, cnd, t)
            return lax.fori_loop(0, 32, bit_body, jnp.full((rows, LANES), INT_MIN, I32))

        def row_total(c):
            return jnp.broadcast_to(jnp.sum(c, axis=1, keepdims=True), c.shape)

        def cand_count_ge(cnd):
            c = jnp.zeros((qb, LANES), I32)
            for k in range(ncand):
                c = c + jnp.where(cand[k] >= cnd, 1, 0)
            return row_total(c)

        thr[...] = kth_largest(cand_count_ge, qb)

        def finish_rows(rblk, bad, full_search):
            r0 = pl.multiple_of(rblk * rb, rb)
            rlane = lax.broadcasted_iota(I32, (rb, tk), 1)

            def count(pred):
                def body(jj, c):
                    off = pl.multiple_of(jj * tk, tk)
                    mk = jnp.where(pred(skey[pl.ds(r0, rb), pl.ds(off, tk)], off), 1, 0)
                    part = mk[:, :LANES]
                    for u in range(1, rep):
                        part = part + mk[:, u * LANES:(u + 1) * LANES]
                    return c + part
                return row_total(lax.fori_loop(0, kd + 1, body, jnp.zeros((rb, LANES), I32)))

            if full_search:
                t = kth_largest(lambda cnd: count(lambda x, off: x >= lanes(cnd)), rb)
            else:
                t = thr[pl.ds(r0, rb), :]
            ts = jnp.maximum(t, INT_MIN + 1)
            thr[pl.ds(r0, rb), :] = ts
            jcut[pl.ds(r0, rb), :] = jnp.full((rb, LANES), 2 ** 30, I32)

            tsl = lanes(ts)
            c_gt = count(lambda x, off: x > tsl)
            c_eq = count(lambda x, off: x == tsl)
            need = n_sel - c_gt
            excess = jnp.max(jnp.where((c_eq > need) & (t != INT_MIN), 1.0, 0.0))

            @pl.when(excess > 0.5)
            def _():
                def jbit(bi, jc):
                    cnd = jc | lax.shift_left(jnp.int32(1), jnp.int32(15) - bi)
                    cl = lanes(cnd)
                    cnt = count(lambda x, off: (x == tsl) & (rlane + off < cl))
                    return jnp.where(cnt <= need, cnd, jc)
                jcut[pl.ds(r0, rb), :] = lax.fori_loop(0, 16, jbit, jnp.zeros((rb, LANES), I32))

            return jnp.maximum(bad, jnp.max(jnp.where(need <= 0, 1.0, 0.0)))

        bad = lax.fori_loop(0, qb // rb, functools.partial(finish_rows, full_search=False), jnp.float32(0.0))

        @pl.when(bad > 0.5)
        def _():
            lax.fori_loop(0, qb // rb, functools.partial(finish_rows, full_search=True), jnp.float32(0.0))

    ts = lanes(thr[...])
    jc = lanes(jcut[...])
    lane = lax.broadcasted_iota(I32, (qb, tk), 1)

    def mask_tile(jj, _):
        off = pl.multiple_of(jj * tk, tk)
        x = skey[:, pl.ds(off, tk)]
        sel = (x > ts) | ((x == ts) & (lane + off < jc))
        mask_ref[0, :, pl.ds(off, tk)] = jnp.where(sel, 0.0, MASKED).astype(mask_ref.dtype)
        return 0

    def blank_tile(jj, _):
        off = pl.multiple_of(jj * tk, tk)
        mask_ref[0, :, pl.ds(off, tk)] = jnp.full((qb, tk), MASKED, mask_ref.dtype)
        return 0

    lax.fori_loop(0, kd + 1, mask_tile, 0)
    lax.fori_loop(kd + 1, nkt, blank_tile, 0)


def _topk_mask(qi, small, kit, *, past, n_sel):
    b, t, _ = qi.shape
    lp = kit.shape[2]
    tk = KEY_TILE
    qb = min(t, tk)
    assert past % tk == 0 and lp % tk == 0 and t % qb == 0
    assert qb == tk or t == qb
    assert lp < 2 ** 16
    kern = functools.partial(_topk_kernel, past=past, qb=qb, tk=tk, rb=min(qb, 128), ncand=TOPK_CANDIDATES,
                             n_sel=n_sel, nkt=lp // tk)
    row = lambda bi, i: (bi, i, 0)
    return pl.pallas_call(
        kern,
        out_shape=jax.ShapeDtypeStruct((b, t, lp), BF16),
        grid=(b, t // qb),
        in_specs=[pl.BlockSpec((1, qb, H_I * D_IDX), row),
                  pl.BlockSpec((1, qb, LANES), row),
                  pl.BlockSpec((1, D_IDX, lp), lambda bi, i: (bi, 0, 0), pipeline_mode=pl.Buffered(1))],
        out_specs=pl.BlockSpec((1, qb, lp), row),
        scratch_shapes=[pltpu.VMEM((qb, lp), I32),
                        pltpu.VMEM((TOPK_CANDIDATES, qb, LANES), I32),
                        pltpu.VMEM((qb, LANES), I32),
                        pltpu.VMEM((qb, LANES), I32),
                        pltpu.VMEM((H_I, qb, D_IDX), BF16),
                        pltpu.VMEM((H_I, qb, LANES), F32)],
        compiler_params=_cparams(("arbitrary", "arbitrary"), 56),
        name="topk_mask",
    )(qi, small, kit)


def _flash_kernel(itab, jtab, rb_ref, q_ref, ma_ref, mb_ref, ka_ref, kb_ref, va_ref, vb_ref, o_ref,
                  madd, m_scr, l_scr, acc, dtile, *, past, qf, tk, nheads):
    b = pl.program_id(0)
    step = pl.program_id(1)
    i = itab[step]
    j = jtab[step]
    hd = HD_A
    rs = min(qf, tk)
    nsub = qf // rs
    kd_last = past // tk + (i * qf + qf - 1) // tk
    blk_b = kd_last - 2 * j
    blk_a = blk_b - 1

    @pl.when((b == 0) & (step == 0))
    def _():
        rr = lax.broadcasted_iota(I32, (rs, tk), 0)
        cc = lax.broadcasted_iota(I32, (rs, tk), 1)
        for slot in (0, 1):
            rel = cc - rr - slot * tk
            n = jnp.abs(rel)
            large = jnp.full((rs, tk), NUM_BUCKETS // 4, I32)
            for th in _T5_THRESHOLDS:
                large = large + jnp.where(n >= th, 1, 0)
            bucket = jnp.where(rel > 0, NUM_BUCKETS // 2, 0) + jnp.where(n < NUM_BUCKETS // 4, n, large)

            def head_body(h, _, bucket=bucket, slot=slot):
                base = rb_ref[_FAR_BUCKET, h]
                tile = jnp.zeros((rs, tk), F32)
                for bk in range(NUM_BUCKETS):
                    if bk != _FAR_BUCKET:
                        tile = jnp.where(bucket == bk, (rb_ref[bk, h] - base) * LOG2E, tile)
                dtile[h, slot] = tile
                return 0

            lax.fori_loop(0, nheads, head_body, 0)

    @pl.when(j == 0)
    def _():
        m_scr[...] = jnp.full(m_scr.shape, MASKED, F32)
        l_scr[...] = jnp.zeros(l_scr.shape, F32)
        acc[...] = jnp.zeros(acc.shape, F32)

    hang = jnp.where(blk_a < 0, MASKED, 0.0)
    madd[:, :tk] = ma_ref[0].astype(F32) + hang
    madd[:, tk:] = mb_ref[0].astype(F32)

    def bias(h, sub, blk):
        back = kd_last - (nsub - 1 - sub) - blk
        on = jnp.where((back == 0) | (back == 1), 1.0, 0.0)
        return dtile[h, jnp.clip(back, 0, 1)] * on

    def heads(near):
        for h in range(nheads):
            head(h, near)

    def head(h, near):
        hs = slice(h * hd, (h + 1) * hd)
        qh = q_ref[0, :, hs]
        sa = lax.dot_general(qh, ka_ref[0, :, hs], _NT, preferred_element_type=F32)
        sb = lax.dot_general(qh, kb_ref[0, :, hs], _NT, preferred_element_type=F32)
        if near:
            rows = []
            for sub in range(nsub):
                r = slice(sub * rs, (sub + 1) * rs)
                rows.append(jnp.concatenate([sa[r] + bias(h, sub, blk_a), sb[r] + bias(h, sub, blk_b)], axis=1))
            s = jnp.concatenate(rows, axis=0) if nsub > 1 else rows[0]
        else:
            s = jnp.concatenate([sa, sb], axis=1)
        s = s + madd[...]
        m_prev = m_scr[h]
        m_new = jnp.maximum(m_prev, jnp.max(s, axis=1, keepdims=True))
        alpha = jnp.exp2(m_prev - m_new)
        p = jnp.exp2(s - jnp.concatenate([m_new] * (2 * tk // LANES), axis=1))
        l_scr[h] = alpha * l_scr[h] + jnp.sum(p, axis=1, keepdims=True)
        m_scr[h] = m_new
        pb = p.astype(BF16)
        pv = (jnp.dot(pb[:, :tk], va_ref[0, :, hs], preferred_element_type=F32)
              + jnp.dot(pb[:, tk:], vb_ref[0, :, hs], preferred_element_type=F32))
        acc[:, hs] = alpha * acc[:, hs] + pv

    near_steps = nsub // 2 + 1

    @pl.when(j < near_steps)
    def _():
        heads(True)

    @pl.when(j >= near_steps)
    def _():
        heads(False)

    @pl.when(j == kd_last // 2)
    def _():
        for h in range(nheads):
            hs = slice(h * hd, (h + 1) * hd)
            o_ref[0, :, hs] = (acc[:, hs] / l_scr[h]).astype(o_ref.dtype)


def _flash(rel_bias, q, mask, kall, vall, *, past):
    b, t, dm = q.shape
    lp = kall.shape[1]
    nheads = dm // HD_A
    tk = KEY_TILE
    qf = min(t, FLASH_ROWS)
    assert past % tk == 0 and lp % tk == 0 and t % qf == 0
    assert qf % tk == 0 or t == qf
    klast = lambda i: past // tk + (i * qf + qf - 1) // tk
    steps = [(i, j) for i in range(t // qf) for j in range(klast(i) // 2 + 1)]
    itab = jnp.asarray([s[0] for s in steps], I32)
    jtab = jnp.asarray([s[1] for s in steps], I32)
    kern = functools.partial(_flash_kernel, past=past, qf=qf, tk=tk, nheads=nheads)
    rs = min(qf, tk)

    def row(bi, s, it, jt):
        return (bi, it[s], 0)

    def tile_a(bi, s, it, jt):
        return (bi, jnp.maximum(klast(it[s]) - 2 * jt[s] - 1, 0), 0)

    def tile_b(bi, s, it, jt):
        return (bi, klast(it[s]) - 2 * jt[s], 0)

    def mask_a(bi, s, it, jt):
        return (bi, it[s], jnp.maximum(klast(it[s]) - 2 * jt[s] - 1, 0))

    def mask_b(bi, s, it, jt):
        return (bi, it[s], klast(it[s]) - 2 * jt[s])

    grid_spec = pltpu.PrefetchScalarGridSpec(
        num_scalar_prefetch=2,
        grid=(b, len(steps)),
        in_specs=[pl.BlockSpec(memory_space=pltpu.SMEM),
                  pl.BlockSpec((1, qf, dm), row),
                  pl.BlockSpec((1, qf, tk), mask_a),
                  pl.BlockSpec((1, qf, tk), mask_b),
                  pl.BlockSpec((1, tk, dm), tile_a),
                  pl.BlockSpec((1, tk, dm), tile_b),
                  pl.BlockSpec((1, tk, dm), tile_a),
                  pl.BlockSpec((1, tk, dm), tile_b)],
        out_specs=pl.BlockSpec((1, qf, dm), row),
        scratch_shapes=[pltpu.VMEM((qf, 2 * tk), F32),
                        pltpu.VMEM((nheads, qf, LANES), F32),
                        pltpu.VMEM((nheads, qf, LANES), F32),
                        pltpu.VMEM((qf, dm), F32),
                        pltpu.VMEM((nheads, 2, rs, tk), F32)])
    return pl.pallas_call(
        kern,
        out_shape=jax.ShapeDtypeStruct((b, t, dm), BF16),
        grid_spec=grid_spec,
        compiler_params=_cparams(("arbitrary", "arbitrary"), 56),
        name="masked_flash",
    )(itab, jtab, rel_bias, q, mask, mask, kall, kall, vall, vall)


def _pad_rows(a, rows):
    return jnp.pad(a, [(0, 0), (rows - a.shape[1], 0)] + [(0, 0)] * (a.ndim - 2))


def _run_group(x, mod, cache_k, cache_v, cache_kidx, st_conv, st_delta, st_ffn, rel_bias, g_final, w):
    b, t, d = x.shape
    nheads = d // HD_A
    sh1, sc1, gt1, sh2, sc2, gt2 = [m[:, None, :] for m in jnp.split(mod, 6, axis=-1)]
    m_rows = b * t

    h = _norm_mod(x, w["g_norm1"], sc1, sh1).reshape(m_rows, d)
    proj = lambda name, dt=F32: _matmul(h, w[name], dt, name="proj_" + name).reshape(b, t, -1)
    qa, qi = proj("w_qa", BF16), proj("w_qi", BF16)
    ka, va = proj("w_ka"), proj("w_va")
    small = proj("w_small")
    qkv_b, z_b, gate_a, gate_b = proj("w_qkvb"), proj("w_z"), proj("w_ga"), proj("w_gb")
    ki = small[..., :D_IDX]

    past = 0 if cache_k is None else cache_k.shape[1]
    l_all = past + t
    tk = KEY_TILE
    lp = -(-l_all // tk) * tk
    if cache_k is None:
        k_all, v_all, ki_all = ka, va, ki
    else:
        k_all = jnp.concatenate([cache_k.reshape(b, past, d), ka], axis=1)
        v_all = jnp.concatenate([cache_v.reshape(b, past, d), va], axis=1)
        ki_all = jnp.concatenate([cache_kidx, ki], axis=1)
    padk = lambda a: jnp.pad(a, ((0, 0), (0, lp - l_all), (0, 0))).astype(BF16)
    k_all, v_all = padk(k_all), padk(v_all)
    kit = jnp.swapaxes(padk(ki_all), 1, 2)
    n_sel = min(TOPK_MAX, l_all // 4)
    mask = _topk_mask(qi, small, kit, past=past, n_sel=n_sel)
    o_a = _flash(rel_bias, qa, mask, k_all, v_all, past=past)

    n = t // CHUNK
    to_rows = lambda v: jnp.swapaxes(v.reshape(b, n, CHUNK, nheads), 2, 3).reshape(b, n, 1, nheads * CHUNK)
    ab = jnp.concatenate([to_rows(small[..., 80:96]), to_rows(small[..., 96:112]),
                          jnp.zeros((b, n, SUBLANES - 2, nheads * CHUNK), F32)], axis=2)
    ogb, new_delta = _delta(qkv_b, _pad_rows(st_conv, SUBLANES), w["conv_b_w"], ab, w["hp_rows"],
                            st_delta, z_b, gate_b, w["g_delta_norm"])
    assert t >= CONVB_WIDTH - 1
    new_conv = qkv_b[:, t - (CONVB_WIDTH - 1):]

    x1 = _merge_out(o_a, gate_a, ogb, w["w_out"], x, gt1)
    h2 = _norm_mod(x1, w["g_norm2"], sc2, sh2)
    y, tail = _ffn(h2, w["w_ffn_gate"], w["w_ffn_up"], w["w_ffn_down"], w["ffn_conv_w"], w["ffn_conv_b"],
                   st_ffn, x1, gt2, g_final)
    new_ffn = tail[:, -1, SUBLANES - (FFN_CONV_WIDTH - 1):]
    kshape = (1, b, t, nheads, HD_A)
    return (y, ka.reshape(kshape), va.reshape(kshape), ki[None], new_conv[None], new_delta[None], new_ffn[None])


def kernel(x_prompt, x_sample, c_prompt, c_sample, cache_k, cache_v, cache_kidx, state_conv_b, state_delta,
           state_ffn_conv, rel_bias, g_final, w_ada, b_ada, g_norm1, w_in, conv_b_w, a_log, dt_bias,
           g_delta_norm, w_out, g_norm2, w_ffn_gate, ffn_conv_w, ffn_conv_b, w_ffn_up, w_ffn_down):
    assert w_ada.shape[0] == 1, "one layer"
    d = x_prompt.shape[-1]
    nheads = d // HD_A
    bp, bs = x_prompt.shape[0], x_sample.shape[0]
    convb = 3 * nheads * DK_B

    wi_ = w_in[0]
    sizes = (d, d, d, H_I * D_IDX, D_IDX, H_I, convb, nheads, nheads, d, d, d)
    offs = [0]
    for s in sizes:
        offs.append(offs[-1] + s)
    col = lambda k: wi_[:, offs[k]:offs[k + 1]]
    w_small = jnp.concatenate([col(4), col(5), col(7), col(8),
                               jnp.zeros((d, LANES - D_IDX - H_I - 2 * nheads), F32)], axis=1)
    bf = lambda a: a.astype(BF16)
    w = dict(
        w_qa=bf(col(0) * (HD_A ** -0.5 * LOG2E)), w_ka=bf(col(1)), w_va=bf(col(2)), w_qi=bf(col(3)), w_small=bf(w_small),
        w_qkvb=bf(col(6)), w_z=bf(col(9)), w_ga=bf(col(10)), w_gb=bf(col(11)),
        g_norm1=g_norm1, g_norm2=g_norm2, conv_b_w=conv_b_w[0], g_delta_norm=g_delta_norm,
        w_out=bf(w_out[0]), w_ffn_gate=bf(w_ffn_gate[0]), w_ffn_up=bf(w_ffn_up[0]),
        w_ffn_down=bf(w_ffn_down[0]), ffn_conv_w=ffn_conv_w[0], ffn_conv_b=ffn_conv_b,
        hp_rows=jnp.concatenate([jnp.repeat(a_log[0], CHUNK)[None], jnp.repeat(dt_bias[0], CHUNK)[None],
                                 jnp.zeros((SUBLANES - 2, nheads * CHUNK), F32)], axis=0),
    )

    c_all = jnp.concatenate([c_prompt, c_sample], axis=0)
    rows = -(-c_all.shape[0] // SUBLANES) * SUBLANES
    c_all = jnp.pad(c_all, ((0, rows - c_all.shape[0]), (0, 0)))
    mod = _adaln(c_all, w_ada[0], b_ada)

    zeros = lambda *s: jnp.zeros(s, F32)
    outs_p = _run_group(x_prompt, mod[:bp], None, None, None,
                        zeros(bp, CONVB_WIDTH - 1, convb), zeros(bp, nheads, DK_B, DV_B),
                        zeros(bp, FFN_CONV_WIDTH - 1, w_ffn_gate.shape[-1]), rel_bias, g_final[None], w)
    outs_s = _run_group(x_sample, mod[bp:bp + bs], cache_k[0], cache_v[0], cache_kidx[0],
                        state_conv_b[0], state_delta[0], state_ffn_conv[0], rel_bias, g_final[None], w)
    return (outs_p[0], outs_s[0]) + tuple(outs_p[1:]) + tuple(outs_s[1:])
```

```python
import functools
import math

import jax
import jax.numpy as jnp
from jax import lax
from jax.experimental import pallas as pl
from jax.experimental.pallas import tpu as pltpu

F32 = jnp.float32
BF16 = jnp.bfloat16
I32 = jnp.int32

CHUNK = 64
HD_A = 128
H_I = 16
D_IDX = 64
TOPK_MAX = 256
NUM_BUCKETS = 32
MAX_DISTANCE = 128
DK_B = 128
DV_B = 128
CONVB_WIDTH = 4
FFN_CONV_WIDTH = 3
EPS = 1e-6
LOG2E = math.log2(math.e)

V7X_VMEM_BYTES = 64 * 1024 * 1024
LANES = 128
SUBLANES = 8
INT_MIN = -(2 ** 31)


def _cparams(sem, vmem_mb):
    assert vmem_mb * 1024 * 1024 < V7X_VMEM_BYTES
    return pltpu.CompilerParams(dimension_semantics=sem, vmem_limit_bytes=vmem_mb * 1024 * 1024)


def _sigmoid(x):
    return 1.0 / (1.0 + jnp.exp(-x))


def _silu(x):
    return x * _sigmoid(x)


def _dot(a, b, dims=None, passes=1):
    dn = dims if dims is not None else (((a.ndim - 1,), (0,)), ((), ()))
    if passes == 6:
        return lax.dot_general(a, b, dn, precision=lax.Precision.HIGHEST, preferred_element_type=F32)
    ah = a.astype(BF16)
    bh = b.astype(BF16)
    out = lax.dot_general(ah, bh, dn, preferred_element_type=F32)
    if passes == 3:
        al = (a - ah.astype(F32)).astype(BF16)
        bl = (b - bh.astype(F32)).astype(BF16)
        out = out + lax.dot_general(ah, bl, dn, preferred_element_type=F32)
        out = out + lax.dot_general(al, bh, dn, preferred_element_type=F32)
    return out


_NT = (((1,), (1,)), ((), ()))
_TN = (((0,), (0,)), ((), ()))


def _adaln_kernel(c_ref, w_ref, b_ref, o_ref):
    o_ref[...] = _dot(_silu(c_ref[...]), w_ref[...]) + b_ref[...]


def _adaln(c, w, b):
    bp, d = c.shape
    n = w.shape[1]
    tn = 1024
    return pl.pallas_call(
        _adaln_kernel,
        out_shape=jax.ShapeDtypeStruct((bp, n), F32),
        grid=(n // tn,),
        in_specs=[pl.BlockSpec((bp, d), lambda j: (0, 0)),
                  pl.BlockSpec((d, tn), lambda j: (0, j)),
                  pl.BlockSpec((1, tn), lambda j: (0, j))],
        out_specs=pl.BlockSpec((bp, tn), lambda j: (0, j)),
        compiler_params=_cparams(("arbitrary",), 40),
        name="adaln",
    )(c, w, b)


def _norm_mod_kernel(x_ref, g_ref, sc_ref, sh_ref, o_ref):
    x = x_ref[...]
    y = x * lax.rsqrt(jnp.mean(x * x, axis=-1, keepdims=True) + EPS)
    o_ref[...] = ((y * g_ref[...]) * (1.0 + sc_ref[...]) + sh_ref[...]).astype(o_ref.dtype)


def _row_tiles(b, t, rows):
    if t >= rows:
        return 1, rows
    bb = max(1, min(b, rows // t))
    while b % bb:
        bb -= 1
    return bb, t


def _norm_mod(x, g, sc, sh):
    b, t, d = x.shape
    bb, tt = _row_tiles(b, t, 512)
    return pl.pallas_call(
        _norm_mod_kernel,
        out_shape=jax.ShapeDtypeStruct((b, t, d), BF16),
        grid=(b // bb, t // tt),
        in_specs=[pl.BlockSpec((bb, tt, d), lambda i, j: (i, j, 0)),
                  pl.BlockSpec((1, d), lambda i, j: (0, 0)),
                  pl.BlockSpec((bb, 1, d), lambda i, j: (i, 0, 0)),
                  pl.BlockSpec((bb, 1, d), lambda i, j: (i, 0, 0))],
        out_specs=pl.BlockSpec((bb, tt, d), lambda i, j: (i, j, 0)),
        compiler_params=_cparams(("arbitrary", "arbitrary"), 32),
        name="norm_mod",
    )(x, g, sc, sh)


def _mm_kernel(a_ref, w_ref, *o_refs):
    acc = jnp.dot(a_ref[...], w_ref[...], preferred_element_type=F32)
    for o_ref in o_refs:
        o_ref[...] = acc.astype(o_ref.dtype)


def _matmul(a, w, out_dtypes=(F32,), name="proj"):
    m, k = a.shape
    n = w.shape[1]
    tm = min(m, 2048)
    tn = min(n, 512)
    return pl.pallas_call(
        _mm_kernel,
        out_shape=tuple(jax.ShapeDtypeStruct((m, n), dt) for dt in out_dtypes),
        grid=(m // tm, n // tn),
        in_specs=[pl.BlockSpec((tm, k), lambda i, j: (i, 0)),
                  pl.BlockSpec((k, tn), lambda i, j: (0, j))],
        out_specs=tuple(pl.BlockSpec((tm, tn), lambda i, j: (i, j)) for _ in out_dtypes),
        compiler_params=_cparams(("arbitrary", "arbitrary"), 48),
        name=name,
    )(a, w)


def _merge_out_kernel(oa_ref, ga_ref, ogb_ref, w_ref, x_ref, gt_ref, o_ref, mixed):
    bb, tt, k = oa_ref.shape

    @pl.when(pl.program_id(2) == 0)
    def _():
        mix = _sigmoid(ga_ref[...]) * oa_ref[...] + ogb_ref[...]
        mixed[...] = mix.reshape(bb * tt, k).astype(BF16)

    y = jnp.dot(mixed[...], w_ref[...], preferred_element_type=F32)
    o_ref[...] = x_ref[...] + gt_ref[...] * y.reshape(bb, tt, -1)


def _merge_out(oa, ga, ogb, w, x, gt):
    b, t, k = oa.shape
    n = w.shape[1]
    bb, tt = _row_tiles(b, t, 512)
    tn = 512
    row = lambda i, s, j: (i, s, 0)
    return pl.pallas_call(
        _merge_out_kernel,
        out_shape=jax.ShapeDtypeStruct((b, t, n), F32),
        grid=(b // bb, t // tt, n // tn),
        in_specs=[pl.BlockSpec((bb, tt, k), row),
                  pl.BlockSpec((bb, tt, k), row),
                  pl.BlockSpec((bb, tt, k), row),
                  pl.BlockSpec((k, tn), lambda i, s, j: (0, j)),
                  pl.BlockSpec((bb, tt, tn), lambda i, s, j: (i, s, j)),
                  pl.BlockSpec((bb, 1, tn), lambda i, s, j: (i, 0, j))],
        out_specs=pl.BlockSpec((bb, tt, tn), lambda i, s, j: (i, s, j)),
        scratch_shapes=[pltpu.VMEM((bb * tt, k), BF16)],
        compiler_params=_cparams(("arbitrary", "arbitrary", "arbitrary"), 40),
        name="merge_out",
    )(oa, ga, ogb, w, x, gt)


def _ffn_kernel(h_ref, wg_ref, wu_ref, wd_ref, cw_ref, cb_ref, st_ref, y_ref, tail_ref, carry):
    s = pl.program_id(1)
    f = pl.program_id(2)
    bb, tt, d = h_ref.shape
    tm = bb * tt
    tf = wg_ref.shape[1]

    h = h_ref[...].reshape(tm, d)
    gate = jnp.dot(h, wg_ref[...], preferred_element_type=F32)
    up = jnp.dot(h, wu_ref[...], preferred_element_type=F32)

    st = st_ref[...]
    first = s == 0

    @pl.when(first)
    def _():
        carry[f] = jnp.zeros((SUBLANES, tf), F32)

    prev = carry[f]
    p0 = jnp.where(first, st[:, 0:1, :], prev[6:7, :][None])
    p1 = jnp.where(first, st[:, 1:2, :], prev[7:8, :][None])
    p0 = jnp.broadcast_to(p0, (bb, tt, tf)).reshape(tm, tf)
    p1 = jnp.broadcast_to(p1, (bb, tt, tf)).reshape(tm, tf)
    tpos = lax.broadcasted_iota(I32, (tm, tf), 0) % tt
    r1 = pltpu.roll(gate, 1, axis=0)
    r2 = pltpu.roll(gate, 2, axis=0)
    x1 = jnp.where(tpos == 0, p1, r1)
    x2 = jnp.where(tpos == 0, p0, jnp.where(tpos == 1, p1, r2))
    cw = cw_ref[...]
    a = cw[0:1] * x2 + cw[1:2] * x1 + cw[2:3] * gate + cb_ref[...]
    act = (_silu(a) * up).astype(BF16)
    contrib = jnp.dot(act, wd_ref[...], preferred_element_type=F32).reshape(bb, tt, d)

    g3 = gate.reshape(bb, tt, tf)
    tail_ref[...] = g3[:, None, tt - SUBLANES:, :]
    carry[f] = gate[tm - SUBLANES:, :]

    @pl.when(f == 0)
    def _():
        y_ref[...] = contrib

    @pl.when(f > 0)
    def _():
        y_ref[...] += contrib


def _ffn(h2, wg, wu, wd, cw, cb, st):
    b, t, d = h2.shape
    dff = wg.shape[1]
    bb, tt = _row_tiles(b, t, 1024)
    tf = 512
    nf = dff // tf
    row = lambda i, s, f: (i, s, 0)
    return pl.pallas_call(
        _ffn_kernel,
        out_shape=(jax.ShapeDtypeStruct((b, t, d), F32),
                   jax.ShapeDtypeStruct((b, t // tt, SUBLANES, dff), F32)),
        grid=(b // bb, t // tt, nf),
        in_specs=[pl.BlockSpec((bb, tt, d), row),
                  pl.BlockSpec((d, tf), lambda i, s, f: (0, f)),
                  pl.BlockSpec((d, tf), lambda i, s, f: (0, f)),
                  pl.BlockSpec((tf, d), lambda i, s, f: (f, 0)),
                  pl.BlockSpec((FFN_CONV_WIDTH, tf), lambda i, s, f: (0, f)),
                  pl.BlockSpec((1, tf), lambda i, s, f: (0, f)),
                  pl.BlockSpec((bb, FFN_CONV_WIDTH - 1, tf), lambda i, s, f: (i, 0, f))],
        out_specs=(pl.BlockSpec((bb, tt, d), row),
                   pl.BlockSpec((bb, 1, SUBLANES, tf), lambda i, s, f: (i, s, 0, f))),
        scratch_shapes=[pltpu.VMEM((nf, SUBLANES, tf), F32)],
        compiler_params=_cparams(("arbitrary", "arbitrary", "arbitrary"), 56),
        name="conv_ffn",
    )(h2, wg, wu, wd, cw, cb, st)


def _res_norm_kernel(x_ref, y_ref, gt_ref, gf_ref, o_ref):
    x = x_ref[...] + gt_ref[...] * y_ref[...]
    o_ref[...] = x * lax.rsqrt(jnp.mean(x * x, axis=-1, keepdims=True) + EPS) * gf_ref[...]


def _res_norm(x1, y, gt, gf):
    b, t, d = x1.shape
    bb, tt = _row_tiles(b, t, 512)
    row = lambda i, j: (i, j, 0)
    return pl.pallas_call(
        _res_norm_kernel,
        out_shape=jax.ShapeDtypeStruct((b, t, d), F32),
        grid=(b // bb, t // tt),
        in_specs=[pl.BlockSpec((bb, tt, d), row),
                  pl.BlockSpec((bb, tt, d), row),
                  pl.BlockSpec((bb, 1, d), lambda i, j: (i, 0, 0)),
                  pl.BlockSpec((1, d), lambda i, j: (0, 0))],
        out_specs=pl.BlockSpec((bb, tt, d), row),
        compiler_params=_cparams(("arbitrary", "arbitrary"), 40),
        name="res_norm",
    )(x1, y, gt, gf)


HEADS_PER_GROUP = 4
GROUP_ROWS = HEADS_PER_GROUP * CHUNK
_DELTA_PASSES = dict(kk=1, inv=1, uw=1, qk=1, state=1, intra=1, upd=1)


def _delta_kernel(x_ref, st_ref, cw_ref, ab_ref, hp_ref, s0_ref, z_ref, gb_ref, gdn_ref,
                  o_ref, s_ref, carry, *, nheads):
    c = pl.program_id(1)
    r = GROUP_ROWS
    hd = DK_B

    @pl.when(c == 0)
    def _():
        carry[...] = st_ref[0]
        s_ref[...] = s0_ref[...]

    prev = carry[...]
    carry[...] = x_ref[0, CHUNK - SUBLANES:, :]

    ri = lax.broadcasted_iota(I32, (r, r), 0)
    ci = lax.broadcasted_iota(I32, (r, r), 1)
    same_head = (ri // CHUNK) == (ci // CHUNK)
    incl = same_head & (ri >= ci)
    strict = same_head & (ri > ci)
    triu = jnp.where(same_head & (ri <= ci), 1.0, 0.0).astype(F32)
    eye = jnp.where(ri == ci, 1.0, 0.0).astype(F32)
    row8 = lax.broadcasted_iota(I32, (SUBLANES, r * 2), 0)
    rid = lax.broadcasted_iota(I32, (LANES, r), 0)

    def conv_silu(col0, width):
        xs = x_ref[0, :, col0:col0 + width]
        pv = prev[:, col0:col0 + width]
        w = cw_ref[:, col0:col0 + width]
        y = xs * w[CONVB_WIDTH - 1:CONVB_WIDTH]
        for k in range(1, CONVB_WIDTH):
            rolled = pltpu.roll(xs, k, axis=0)
            head = jnp.where(row8[:, :width] < k, pltpu.roll(pv, k, axis=0), rolled[:SUBLANES])
            shifted = jnp.concatenate([head, rolled[SUBLANES:]], axis=0)
            y = y + shifted * w[CONVB_WIDTH - 1 - k:CONVB_WIDTH - k]
        return _silu(y)

    def l2n(v):
        return v * lax.rsqrt(jnp.sum(v * v, axis=-1, keepdims=True) + EPS)

    def stack(y, fn):
        return jnp.concatenate([fn(y[:, j * hd:(j + 1) * hd]) for j in range(HEADS_PER_GROUP)], axis=0)

    ngroups = nheads // HEADS_PER_GROUP
    gw = HEADS_PER_GROUP * hd
    pp = _DELTA_PASSES

    def prep(g):
        qs = stack(conv_silu(g * gw, gw), lambda v: l2n(v) * (DK_B ** -0.5))
        ks = stack(conv_silu(nheads * hd + g * gw, gw), l2n)
        vs = stack(conv_silu(2 * nheads * hd + g * gw, gw), lambda v: v)
        a_row = ab_ref[0, 0, 0:1, g * r:(g + 1) * r]
        b_row = ab_ref[0, 0, 1:2, g * r:(g + 1) * r]
        alog = hp_ref[0:1, g * r:(g + 1) * r]
        dtb = hp_ref[1:2, g * r:(g + 1) * r]
        xsp = a_row + dtb
        softplus = jnp.maximum(xsp, 0.0) + jnp.log1p(jnp.exp(-jnp.abs(xsp)))
        g_row = -jnp.exp(alog) * softplus
        beta_row = _sigmoid(b_row)
        gc_row = _dot(jnp.broadcast_to(g_row, (SUBLANES, r)), triu, passes=6)[0:1]
        rows = jnp.where(rid == 0, beta_row, jnp.where(rid == 1, gc_row, 0.0))
        cols = rows.T
        beta_col = cols[:, 0:1]
        gc_col = cols[:, 1:2]
        gl_col = jnp.concatenate(
            [jnp.broadcast_to(cols[(j + 1) * CHUNK - 1:(j + 1) * CHUNK, 1:2], (CHUNK, 1))
             for j in range(HEADS_PER_GROUP)], axis=0)
        decay = jnp.where(incl, jnp.exp(gc_col - gc_row), 0.0)
        egc = jnp.exp(gc_col)
        kb = ks * beta_col
        return dict(qs=qs, ks=ks, kb=kb, rhs=jnp.concatenate([vs * beta_col, kb * egc], axis=1),
                    decay=decay, qg=qs * egc, kd=ks * jnp.exp(gl_col - gc_col), gl_col=gl_col)

    gs = [prep(g) for g in range(ngroups)]
    ms = [jnp.where(strict, _dot(d["kb"], d["ks"], _NT, passes=pp["kk"]) * d["decay"], 0.0) for d in gs]

    invs = [eye - jnp.where((ri // 2 == ci // 2), m, 0.0) for m in ms]
    bs = 2
    while bs < CHUNK:
        lower = ((ri // (2 * bs)) == (ci // (2 * bs))) & ((ri % (2 * bs)) >= bs) & ((ci % (2 * bs)) < bs)
        xc = [_dot(inv, jnp.where(lower, m, 0.0), passes=pp["inv"]) for inv, m in zip(invs, ms)]
        invs = [inv - _dot(t, inv, passes=pp["inv"]) for inv, t in zip(invs, xc)]
        bs *= 2

    uws = [_dot(inv, d["rhs"], passes=pp["uw"]) for inv, d in zip(invs, gs)]
    attns = [jnp.where(incl, _dot(d["qs"], d["ks"], _NT, passes=pp["qk"]) * d["decay"], 0.0) for d in gs]

    vnews, qstates = [], []
    for g in range(ngroups):
        vnew, qstate = [], []
        for j in range(HEADS_PER_GROUP):
            sl = slice(j * CHUNK, (j + 1) * CHUNK)
            wq = jnp.concatenate([uws[g][sl, hd:], gs[g]["qg"][sl]], axis=0)
            rs = _dot(wq, s_ref[0, g * HEADS_PER_GROUP + j], passes=pp["state"])
            vnew.append(uws[g][sl, :hd] - rs[:CHUNK])
            qstate.append(rs[CHUNK:])
        vnews.append(vnew)
        qstates.append(qstate)
    intras = [_dot(attns[g], jnp.concatenate(vnews[g], axis=0), passes=pp["intra"]) for g in range(ngroups)]

    for g in range(ngroups):
        for j in range(HEADS_PER_GROUP):
            hidx = g * HEADS_PER_GROUP + j
            sl = slice(j * CHUNK, (j + 1) * CHUNK)
            hs = slice(hidx * hd, (hidx + 1) * hd)
            o = qstates[g][j] + intras[g][sl]
            glast = jnp.exp(gs[g]["gl_col"][j * CHUNK:j * CHUNK + 1, :])
            upd = _dot(gs[g]["kd"][sl], vnews[g][j], _TN, passes=pp["upd"])
            s_ref[0, hidx] = s_ref[0, hidx] * glast + upd
            on = o * lax.rsqrt(jnp.mean(o * o, axis=-1, keepdims=True) + EPS) * gdn_ref[...]
            o_ref[0, :, hs] = on * _silu(z_ref[0, :, hs]) * _sigmoid(gb_ref[0, :, hs])


def _delta(xqkv, st8, cw, ab, hp, s0, z, gb, gdn):
    b, t, cdim = xqkv.shape
    nheads = s0.shape[1]
    dm = z.shape[2]
    n = t // CHUNK
    kern = functools.partial(_delta_kernel, nheads=nheads)
    return pl.pallas_call(
        kern,
        out_shape=(jax.ShapeDtypeStruct((b, t, dm), F32),
                   jax.ShapeDtypeStruct(s0.shape, F32)),
        grid=(b, n),
        in_specs=[pl.BlockSpec((1, CHUNK, cdim), lambda i, c: (i, c, 0)),
                  pl.BlockSpec((1, SUBLANES, cdim), lambda i, c: (i, 0, 0)),
                  pl.BlockSpec((CONVB_WIDTH, cdim), lambda i, c: (0, 0)),
                  pl.BlockSpec((1, 1, SUBLANES, nheads * CHUNK), lambda i, c: (i, c, 0, 0)),
                  pl.BlockSpec((SUBLANES, nheads * CHUNK), lambda i, c: (0, 0)),
                  pl.BlockSpec((1, nheads, DK_B, DV_B), lambda i, c: (i, 0, 0, 0)),
                  pl.BlockSpec((1, CHUNK, dm), lambda i, c: (i, c, 0)),
                  pl.BlockSpec((1, CHUNK, dm), lambda i, c: (i, c, 0)),
                  pl.BlockSpec((1, DV_B), lambda i, c: (0, 0))],
        out_specs=(pl.BlockSpec((1, CHUNK, dm), lambda i, c: (i, c, 0)),
                   pl.BlockSpec((1, nheads, DK_B, DV_B), lambda i, c: (i, 0, 0, 0))),
        scratch_shapes=[pltpu.VMEM((SUBLANES, cdim), F32)],
        compiler_params=_cparams(("arbitrary", "arbitrary"), 40),
        name="gated_delta",
    )(xqkv, st8, cw, ab, hp, s0, z, gb, gdn)


_T5_THRESHOLDS = tuple(math.ceil(8 * 2 ** (k / 2)) for k in range(1, 8))
_FAR_BUCKET = NUM_BUCKETS // 2 - 1


MASKED = -1e30
TOPK_CANDIDATES = 12
KEY_TILE = 256
FLASH_ROWS = 512


def _topk_kernel(qi_ref, sm_ref, kit_ref, mask_ref, skey, cand, thr, jcut, qis, wib,
                 *, past, qb, tk, rb, ncand, n_sel, nkt):
    i = pl.program_id(1)
    rep = tk // LANES
    kd = past // tk + i * (qb // tk)
    q0 = past + i * qb

    def lanes(v):
        return jnp.concatenate([v] * rep, axis=1) if rep > 1 else v

    if True:
        sm = sm_ref[0]
        for h in range(H_I):
            qis[h] = qi_ref[0, :, h * D_IDX:(h + 1) * D_IDX]
            wcol = sm[:, D_IDX + h:D_IDX + h + 1] * ((H_I * D_IDX) ** -0.5)
            wib[h] = jnp.broadcast_to(wcol, (qb, LANES))

        qpos = q0 + lax.broadcasted_iota(I32, (qb, tk), 0)
        lim = (qpos // CHUNK + 1) * CHUNK
        lane = lax.broadcasted_iota(I32, (qb, tk), 1)

        def score_tile(j, _):
            off = pl.multiple_of(j * tk, tk)
            kit = kit_ref[0, :, pl.ds(off, tk)]
            sc = jnp.zeros((qb, tk), F32)
            for h in range(H_I):
                d = jnp.dot(qis[h], kit, preferred_element_type=F32)
                sc = sc + jnp.maximum(d, 0.0) * lanes(wib[h])
            bits = pltpu.bitcast(sc + 0.0, I32)
            key = bits ^ ((bits >> 31) & 0x7FFFFFFF)
            key = jnp.where(lane + off < lim, key, INT_MIN)
            skey[:, pl.ds(off, tk)] = key
            for u in range(rep):
                x = key[:, u * LANES:(u + 1) * LANES]
                for c in range(ncand):
                    a = cand[c]
                    cand[c] = jnp.maximum(a, x)
                    x = jnp.minimum(a, x)
            return 0

        cand[...] = jnp.full(cand.shape, INT_MIN, I32)
        lax.fori_loop(0, kd + 1, score_tile, 0)

        def kth_largest(count_ge, rows):
            def bit_body(bi, t):
                cnd = t + lax.shift_left(jnp.int32(1), jnp.int32(31) - bi)
                return jnp.where(count_ge(cnd) >= n_sel, cnd, t)
            return lax.fori_loop(0, 32, bit_body, jnp.full((rows, LANES), INT_MIN, I32))

        def row_total(c):
            return jnp.broadcast_to(jnp.sum(c, axis=1, keepdims=True), c.shape)

        def cand_count_ge(cnd):
            c = jnp.zeros((qb, LANES), I32)
            for k in range(ncand):
                c = c + jnp.where(cand[k] >= cnd, 1, 0)
            return row_total(c)

        thr[...] = kth_largest(cand_count_ge, qb)

        def finish_rows(rblk, bad, full_search):
            r0 = pl.multiple_of(rblk * rb, rb)
            rlane = lax.broadcasted_iota(I32, (rb, tk), 1)

            def count(pred):
                def body(jj, c):
                    off = pl.multiple_of(jj * tk, tk)
                    mk = jnp.where(pred(skey[pl.ds(r0, rb), pl.ds(off, tk)], off), 1, 0)
                    part = mk[:, :LANES]
                    for u in range(1, rep):
                        part = part + mk[:, u * LANES:(u + 1) * LANES]
                    return c + part
                return row_total(lax.fori_loop(0, kd + 1, body, jnp.zeros((rb, LANES), I32)))

            if full_search:
                t = kth_largest(lambda cnd: count(lambda x, off: x >= lanes(cnd)), rb)
            else:
                t = thr[pl.ds(r0, rb), :]
            ts = jnp.maximum(t, INT_MIN + 1)
            thr[pl.ds(r0, rb), :] = ts
            jcut[pl.ds(r0, rb), :] = jnp.full((rb, LANES), 2 ** 30, I32)

            tsl = lanes(ts)
            c_gt = count(lambda x, off: x > tsl)
            c_eq = count(lambda x, off: x == tsl)
            need = n_sel - c_gt
            excess = jnp.max(jnp.where((c_eq > need) & (t != INT_MIN), 1.0, 0.0))

            @pl.when(excess > 0.5)
            def _():
                def jbit(bi, jc):
                    cnd = jc | lax.shift_left(jnp.int32(1), jnp.int32(15) - bi)
                    cl = lanes(cnd)
                    cnt = count(lambda x, off: (x == tsl) & (rlane + off < cl))
                    return jnp.where(cnt <= need, cnd, jc)
                jcut[pl.ds(r0, rb), :] = lax.fori_loop(0, 16, jbit, jnp.zeros((rb, LANES), I32))

            return jnp.maximum(bad, jnp.max(jnp.where(need <= 0, 1.0, 0.0)))

        bad = lax.fori_loop(0, qb // rb, functools.partial(finish_rows, full_search=False), jnp.float32(0.0))

        @pl.when(bad > 0.5)
        def _():
            lax.fori_loop(0, qb // rb, functools.partial(finish_rows, full_search=True), jnp.float32(0.0))

    ts = lanes(thr[...])
    jc = lanes(jcut[...])
    lane = lax.broadcasted_iota(I32, (qb, tk), 1)

    def mask_tile(jj, _):
        off = pl.multiple_of(jj * tk, tk)
        x = skey[:, pl.ds(off, tk)]
        sel = (x > ts) | ((x == ts) & (lane + off < jc))
        mask_ref[0, :, pl.ds(off, tk)] = jnp.where(sel, 0.0, MASKED).astype(mask_ref.dtype)
        return 0

    def blank_tile(jj, _):
        off = pl.multiple_of(jj * tk, tk)
        mask_ref[0, :, pl.ds(off, tk)] = jnp.full((qb, tk), MASKED, mask_ref.dtype)
        return 0

    lax.fori_loop(0, kd + 1, mask_tile, 0)
    lax.fori_loop(kd + 1, nkt, blank_tile, 0)


def _topk_mask(qi, small, kit, *, past, n_sel):
    b, t, _ = qi.shape
    lp = kit.shape[2]
    tk = KEY_TILE
    qb = min(t, tk)
    assert past % tk == 0 and lp % tk == 0 and t % qb == 0
    assert qb == tk or t == qb
    assert lp < 2 ** 16
    kern = functools.partial(_topk_kernel, past=past, qb=qb, tk=tk, rb=min(qb, 128), ncand=TOPK_CANDIDATES,
                             n_sel=n_sel, nkt=lp // tk)
    row = lambda bi, i: (bi, i, 0)
    return pl.pallas_call(
        kern,
        out_shape=jax.ShapeDtypeStruct((b, t, lp), BF16),
        grid=(b, t // qb),
        in_specs=[pl.BlockSpec((1, qb, H_I * D_IDX), row),
                  pl.BlockSpec((1, qb, LANES), row),
                  pl.BlockSpec((1, D_IDX, lp), lambda bi, i: (bi, 0, 0), pipeline_mode=pl.Buffered(1))],
        out_specs=pl.BlockSpec((1, qb, lp), row),
        scratch_shapes=[pltpu.VMEM((qb, lp), I32),
                        pltpu.VMEM((TOPK_CANDIDATES, qb, LANES), I32),
                        pltpu.VMEM((qb, LANES), I32),
                        pltpu.VMEM((qb, LANES), I32),
                        pltpu.VMEM((H_I, qb, D_IDX), BF16),
                        pltpu.VMEM((H_I, qb, LANES), F32)],
        compiler_params=_cparams(("arbitrary", "arbitrary"), 56),
        name="topk_mask",
    )(qi, small, kit)


def _flash_kernel(itab, jtab, rb_ref, q_ref, ma_ref, mb_ref, ka_ref, kb_ref, va_ref, vb_ref, o_ref,
                  madd, m_scr, l_scr, acc, dtile, *, past, qf, tk, nheads):
    b = pl.program_id(0)
    step = pl.program_id(1)
    i = itab[step]
    j = jtab[step]
    hd = HD_A
    rs = min(qf, tk)
    nsub = qf // rs
    kd_last = past // tk + (i * qf + qf - 1) // tk
    blk_b = kd_last - 2 * j
    blk_a = blk_b - 1

    @pl.when((b == 0) & (step == 0))
    def _():
        rr = lax.broadcasted_iota(I32, (rs, tk), 0)
        cc = lax.broadcasted_iota(I32, (rs, tk), 1)
        for slot in (0, 1):
            rel = cc - rr - slot * tk
            n = jnp.abs(rel)
            large = jnp.full((rs, tk), NUM_BUCKETS // 4, I32)
            for th in _T5_THRESHOLDS:
                large = large + jnp.where(n >= th, 1, 0)
            bucket = jnp.where(rel > 0, NUM_BUCKETS // 2, 0) + jnp.where(n < NUM_BUCKETS // 4, n, large)

            def head_body(h, _, bucket=bucket, slot=slot):
                base = rb_ref[_FAR_BUCKET, h]
                tile = jnp.zeros((rs, tk), F32)
                for bk in range(NUM_BUCKETS):
                    if bk != _FAR_BUCKET:
                        tile = jnp.where(bucket == bk, (rb_ref[bk, h] - base) * LOG2E, tile)
                dtile[h, slot] = tile
                return 0

            lax.fori_loop(0, nheads, head_body, 0)

    @pl.when(j == 0)
    def _():
        m_scr[...] = jnp.full(m_scr.shape, MASKED, F32)
        l_scr[...] = jnp.zeros(l_scr.shape, F32)
        acc[...] = jnp.zeros(acc.shape, F32)

    hang = jnp.where(blk_a < 0, MASKED, 0.0)
    madd[:, :tk] = ma_ref[0].astype(F32) + hang
    madd[:, tk:] = mb_ref[0].astype(F32)

    def bias(h, sub, blk):
        back = kd_last - (nsub - 1 - sub) - blk
        on = jnp.where((back == 0) | (back == 1), 1.0, 0.0)
        return dtile[h, jnp.clip(back, 0, 1)] * on

    def heads(near):
        for h in range(nheads):
            head(h, near)

    def head(h, near):
        hs = slice(h * hd, (h + 1) * hd)
        qh = q_ref[0, :, hs]
        sa = lax.dot_general(qh, ka_ref[0, :, hs], _NT, preferred_element_type=F32)
        sb = lax.dot_general(qh, kb_ref[0, :, hs], _NT, preferred_element_type=F32)
        if near:
            rows = []
            for sub in range(nsub):
                r = slice(sub * rs, (sub + 1) * rs)
                rows.append(jnp.concatenate([sa[r] + bias(h, sub, blk_a), sb[r] + bias(h, sub, blk_b)], axis=1))
            s = jnp.concatenate(rows, axis=0) if nsub > 1 else rows[0]
        else:
            s = jnp.concatenate([sa, sb], axis=1)
        s = s + madd[...]
        m_prev = m_scr[h]
        m_new = jnp.maximum(m_prev, jnp.max(s, axis=1, keepdims=True))
        alpha = jnp.exp2(m_prev - m_new)
        p = jnp.exp2(s - jnp.concatenate([m_new] * (2 * tk // LANES), axis=1))
        l_scr[h] = alpha * l_scr[h] + jnp.sum(p, axis=1, keepdims=True)
        m_scr[h] = m_new
        pb = p.astype(BF16)
        pv = (jnp.dot(pb[:, :tk], va_ref[0, :, hs], preferred_element_type=F32)
              + jnp.dot(pb[:, tk:], vb_ref[0, :, hs], preferred_element_type=F32))
        acc[:, hs] = alpha * acc[:, hs] + pv

    near_steps = nsub // 2 + 1

    @pl.when(j < near_steps)
    def _():
        heads(True)

    @pl.when(j >= near_steps)
    def _():
        heads(False)

    @pl.when(j == kd_last // 2)
    def _():
        for h in range(nheads):
            hs = slice(h * hd, (h + 1) * hd)
            o_ref[0, :, hs] = (acc[:, hs] / l_scr[h]).astype(o_ref.dtype)


def _flash(rel_bias, q, mask, kall, vall, *, past):
    b, t, dm = q.shape
    lp = kall.shape[1]
    nheads = dm // HD_A
    tk = KEY_TILE
    qf = min(t, FLASH_ROWS)
    assert past % tk == 0 and lp % tk == 0 and t % qf == 0
    assert qf % tk == 0 or t == qf
    klast = lambda i: past // tk + (i * qf + qf - 1) // tk
    steps = [(i, j) for i in range(t // qf) for j in range(klast(i) // 2 + 1)]
    itab = jnp.asarray([s[0] for s in steps], I32)
    jtab = jnp.asarray([s[1] for s in steps], I32)
    kern = functools.partial(_flash_kernel, past=past, qf=qf, tk=tk, nheads=nheads)
    rs = min(qf, tk)

    def row(bi, s, it, jt):
        return (bi, it[s], 0)

    def tile_a(bi, s, it, jt):
        return (bi, jnp.maximum(klast(it[s]) - 2 * jt[s] - 1, 0), 0)

    def tile_b(bi, s, it, jt):
        return (bi, klast(it[s]) - 2 * jt[s], 0)

    def mask_a(bi, s, it, jt):
        return (bi, it[s], jnp.maximum(klast(it[s]) - 2 * jt[s] - 1, 0))

    def mask_b(bi, s, it, jt):
        return (bi, it[s], klast(it[s]) - 2 * jt[s])

    grid_spec = pltpu.PrefetchScalarGridSpec(
        num_scalar_prefetch=2,
        grid=(b, len(steps)),
        in_specs=[pl.BlockSpec(memory_space=pltpu.SMEM),
                  pl.BlockSpec((1, qf, dm), row),
                  pl.BlockSpec((1, qf, tk), mask_a),
                  pl.BlockSpec((1, qf, tk), mask_b),
                  pl.BlockSpec((1, tk, dm), tile_a),
                  pl.BlockSpec((1, tk, dm), tile_b),
                  pl.BlockSpec((1, tk, dm), tile_a),
                  pl.BlockSpec((1, tk, dm), tile_b)],
        out_specs=pl.BlockSpec((1, qf, dm), row),
        scratch_shapes=[pltpu.VMEM((qf, 2 * tk), F32),
                        pltpu.VMEM((nheads, qf, LANES), F32),
                        pltpu.VMEM((nheads, qf, LANES), F32),
                        pltpu.VMEM((qf, dm), F32),
                        pltpu.VMEM((nheads, 2, rs, tk), F32)])
    return pl.pallas_call(
        kern,
        out_shape=jax.ShapeDtypeStruct((b, t, dm), BF16),
        grid_spec=grid_spec,
        compiler_params=_cparams(("arbitrary", "arbitrary"), 56),
        name="masked_flash",
    )(itab, jtab, rel_bias, q, mask, mask, kall, kall, vall, vall)


def _pad_rows(a, rows):
    return jnp.pad(a, [(0, 0), (rows - a.shape[1], 0)] + [(0, 0)] * (a.ndim - 2))


def _run_group(x, mod, cache_k, cache_v, cache_kidx, st_conv, st_delta, st_ffn, rel_bias, g_final, w):
    b, t, d = x.shape
    nheads = d // HD_A
    sh1, sc1, gt1, sh2, sc2, gt2 = [m[:, None, :] for m in jnp.split(mod, 6, axis=-1)]
    m_rows = b * t

    h = _norm_mod(x, w["g_norm1"], sc1, sh1).reshape(m_rows, d)
    def proj(name, *dts):
        outs = _matmul(h, w[name], dts or (F32,), name="proj_" + name)
        outs = [o.reshape(b, t, -1) for o in outs]
        return outs if len(outs) > 1 else outs[0]

    qa, qi = proj("w_qa", BF16), proj("w_qi", BF16)
    (ka, ka16), (va, va16) = proj("w_ka", F32, BF16), proj("w_va", F32, BF16)
    small = proj("w_small")
    qkv_b, z_b, gate_a, gate_b = proj("w_qkvb"), proj("w_z"), proj("w_ga"), proj("w_gb")
    ki = small[..., :D_IDX]

    past = 0 if cache_k is None else cache_k.shape[1]
    l_all = past + t
    tk = KEY_TILE
    lp = -(-l_all // tk) * tk
    if cache_k is None:
        k_all, v_all, ki_all = ka16, va16, ki
    else:
        k_all = jnp.concatenate([cache_k.reshape(b, past, d).astype(BF16), ka16], axis=1)
        v_all = jnp.concatenate([cache_v.reshape(b, past, d).astype(BF16), va16], axis=1)
        ki_all = jnp.concatenate([cache_kidx, ki], axis=1)
    padk = lambda a: jnp.pad(a, ((0, 0), (0, lp - l_all), (0, 0))).astype(BF16)
    k_all, v_all = padk(k_all), padk(v_all)
    kit = jnp.swapaxes(padk(ki_all), 1, 2)
    n_sel = min(TOPK_MAX, l_all // 4)
    mask = _topk_mask(qi, small, kit, past=past, n_sel=n_sel)
    o_a = _flash(rel_bias, qa, mask, k_all, v_all, past=past)

    n = t // CHUNK
    to_rows = lambda v: jnp.swapaxes(v.reshape(b, n, CHUNK, nheads), 2, 3).reshape(b, n, 1, nheads * CHUNK)
    ab = jnp.concatenate([to_rows(small[..., 80:96]), to_rows(small[..., 96:112]),
                          jnp.zeros((b, n, SUBLANES - 2, nheads * CHUNK), F32)], axis=2)
    ogb, new_delta = _delta(qkv_b, _pad_rows(st_conv, SUBLANES), w["conv_b_w"], ab, w["hp_rows"],
                            st_delta, z_b, gate_b, w["g_delta_norm"])
    assert t >= CONVB_WIDTH - 1
    new_conv = qkv_b[:, t - (CONVB_WIDTH - 1):]

    x1 = _merge_out(o_a, gate_a, ogb, w["w_out"], x, gt1)
    h2 = _norm_mod(x1, w["g_norm2"], sc2, sh2)
    ffn, tail = _ffn(h2, w["w_ffn_gate"], w["w_ffn_up"], w["w_ffn_down"], w["ffn_conv_w"], w["ffn_conv_b"], st_ffn)
    y = _res_norm(x1, ffn, gt2, g_final)
    new_ffn = tail[:, -1, SUBLANES - (FFN_CONV_WIDTH - 1):]
    kshape = (1, b, t, nheads, HD_A)
    return (y, ka.reshape(kshape), va.reshape(kshape), ki[None], new_conv[None], new_delta[None], new_ffn[None])


def kernel(x_prompt, x_sample, c_prompt, c_sample, cache_k, cache_v, cache_kidx, state_conv_b, state_delta,
           state_ffn_conv, rel_bias, g_final, w_ada, b_ada, g_norm1, w_in, conv_b_w, a_log, dt_bias,
           g_delta_norm, w_out, g_norm2, w_ffn_gate, ffn_conv_w, ffn_conv_b, w_ffn_up, w_ffn_down):
    assert w_ada.shape[0] == 1, "one layer"
    d = x_prompt.shape[-1]
    nheads = d // HD_A
    bp, bs = x_prompt.shape[0], x_sample.shape[0]
    convb = 3 * nheads * DK_B

    wi_ = w_in[0]
    sizes = (d, d, d, H_I * D_IDX, D_IDX, H_I, convb, nheads, nheads, d, d, d)
    offs = [0]
    for s in sizes:
        offs.append(offs[-1] + s)
    col = lambda k: wi_[:, offs[k]:offs[k + 1]]
    w_small = jnp.concatenate([col(4), col(5), col(7), col(8),
                               jnp.zeros((d, LANES - D_IDX - H_I - 2 * nheads), F32)], axis=1)
    bf = lambda a: a.astype(BF16)
    w = dict(
        w_qa=bf(col(0) * (HD_A ** -0.5 * LOG2E)), w_ka=bf(col(1)), w_va=bf(col(2)), w_qi=bf(col(3)), w_small=bf(w_small),
        w_qkvb=bf(col(6)), w_z=bf(col(9)), w_ga=bf(col(10)), w_gb=bf(col(11)),
        g_norm1=g_norm1, g_norm2=g_norm2, conv_b_w=conv_b_w[0], g_delta_norm=g_delta_norm,
        w_out=bf(w_out[0]), w_ffn_gate=bf(w_ffn_gate[0]), w_ffn_up=bf(w_ffn_up[0]),
        w_ffn_down=bf(w_ffn_down[0]), ffn_conv_w=ffn_conv_w[0], ffn_conv_b=ffn_conv_b,
        hp_rows=jnp.concatenate([jnp.repeat(a_log[0], CHUNK)[None], jnp.repeat(dt_bias[0], CHUNK)[None],
                                 jnp.zeros((SUBLANES - 2, nheads * CHUNK), F32)], axis=0),
    )

    c_all = jnp.concatenate([c_prompt, c_sample], axis=0)
    rows = -(-c_all.shape[0] // SUBLANES) * SUBLANES
    c_all = jnp.pad(c_all, ((0, rows - c_all.shape[0]), (0, 0)))
    mod = _adaln(c_all, w_ada[0], b_ada)

    zeros = lambda *s: jnp.zeros(s, F32)
    outs_p = _run_group(x_prompt, mod[:bp], None, None, None,
                        zeros(bp, CONVB_WIDTH - 1, convb), zeros(bp, nheads, DK_B, DV_B),
                        zeros(bp, FFN_CONV_WIDTH - 1, w_ffn_gate.shape[-1]), rel_bias, g_final[None], w)
    outs_s = _run_group(x_sample, mod[bp:bp + bs], cache_k[0], cache_v[0], cache_kidx[0],
                        state_conv_b[0], state_delta[0], state_ffn_conv[0], rel_bias, g_final[None], w)
    return (outs_p[0], outs_s[0]) + tuple(outs_p[1:]) + tuple(outs_s[1:])
```

```python
import functools
import math

import jax
import jax.numpy as jnp
from jax import lax
from jax.experimental import pallas as pl
from jax.experimental.pallas import tpu as pltpu

F32 = jnp.float32
BF16 = jnp.bfloat16
I32 = jnp.int32

CHUNK = 64
HD_A = 128
H_I = 16
D_IDX = 64
TOPK_MAX = 256
NUM_BUCKETS = 32
MAX_DISTANCE = 128
DK_B = 128
DV_B = 128
CONVB_WIDTH = 4
FFN_CONV_WIDTH = 3
EPS = 1e-6
LOG2E = math.log2(math.e)

V7X_VMEM_BYTES = 64 * 1024 * 1024
LANES = 128
SUBLANES = 8
INT_MIN = -(2 ** 31)


def _cparams(sem, vmem_mb):
    assert vmem_mb * 1024 * 1024 < V7X_VMEM_BYTES
    return pltpu.CompilerParams(dimension_semantics=sem, vmem_limit_bytes=vmem_mb * 1024 * 1024)


def _sigmoid(x):
    return 1.0 / (1.0 + jnp.exp(-x))


def _silu(x):
    return x * _sigmoid(x)


def _dot(a, b, dims=None, passes=1):
    dn = dims if dims is not None else (((a.ndim - 1,), (0,)), ((), ()))
    if passes == 6:
        return lax.dot_general(a, b, dn, precision=lax.Precision.HIGHEST, preferred_element_type=F32)
    ah = a.astype(BF16)
    bh = b.astype(BF16)
    out = lax.dot_general(ah, bh, dn, preferred_element_type=F32)
    if passes == 3:
        al = (a - ah.astype(F32)).astype(BF16)
        bl = (b - bh.astype(F32)).astype(BF16)
        out = out + lax.dot_general(ah, bl, dn, preferred_element_type=F32)
        out = out + lax.dot_general(al, bh, dn, preferred_element_type=F32)
    return out


_NT = (((1,), (1,)), ((), ()))
_TN = (((0,), (0,)), ((), ()))


def _adaln_kernel(c_ref, w_ref, b_ref, o_ref):
    o_ref[...] = _dot(_silu(c_ref[...]), w_ref[...]) + b_ref[...]


def _adaln(c, w, b):
    bp, d = c.shape
    n = w.shape[1]
    tn = 1024
    return pl.pallas_call(
        _adaln_kernel,
        out_shape=jax.ShapeDtypeStruct((bp, n), F32),
        grid=(n // tn,),
        in_specs=[pl.BlockSpec((bp, d), lambda j: (0, 0)),
                  pl.BlockSpec((d, tn), lambda j: (0, j)),
                  pl.BlockSpec((1, tn), lambda j: (0, j))],
        out_specs=pl.BlockSpec((bp, tn), lambda j: (0, j)),
        compiler_params=_cparams(("arbitrary",), 40),
        name="adaln",
    )(c, w, b)


def _norm_mod_kernel(x_ref, g_ref, sc_ref, sh_ref, o_ref):
    x = x_ref[...]
    y = x * lax.rsqrt(jnp.mean(x * x, axis=-1, keepdims=True) + EPS)
    o_ref[...] = ((y * g_ref[...]) * (1.0 + sc_ref[...]) + sh_ref[...]).astype(o_ref.dtype)


def _row_tiles(b, t, rows):
    if t >= rows:
        return 1, rows
    bb = max(1, min(b, rows // t))
    while b % bb:
        bb -= 1
    return bb, t


def _norm_mod(x, g, sc, sh):
    b, t, d = x.shape
    bb, tt = _row_tiles(b, t, 512)
    return pl.pallas_call(
        _norm_mod_kernel,
        out_shape=jax.ShapeDtypeStruct((b, t, d), BF16),
        grid=(b // bb, t // tt),
        in_specs=[pl.BlockSpec((bb, tt, d), lambda i, j: (i, j, 0)),
                  pl.BlockSpec((1, d), lambda i, j: (0, 0)),
                  pl.BlockSpec((bb, 1, d), lambda i, j: (i, 0, 0)),
                  pl.BlockSpec((bb, 1, d), lambda i, j: (i, 0, 0))],
        out_specs=pl.BlockSpec((bb, tt, d), lambda i, j: (i, j, 0)),
        compiler_params=_cparams(("arbitrary", "arbitrary"), 32),
        name="norm_mod",
    )(x, g, sc, sh)


def _mm_kernel(a_ref, w_ref, *o_refs):
    acc = jnp.dot(a_ref[...], w_ref[...], preferred_element_type=F32)
    for o_ref in o_refs:
        o_ref[...] = acc.astype(o_ref.dtype)


def _matmul(a, w, out_dtypes=(F32,), name="proj"):
    m, k = a.shape
    n = w.shape[1]
    tm = min(m, 2048)
    tn = min(n, 512)
    return pl.pallas_call(
        _mm_kernel,
        out_shape=tuple(jax.ShapeDtypeStruct((m, n), dt) for dt in out_dtypes),
        grid=(m // tm, n // tn),
        in_specs=[pl.BlockSpec((tm, k), lambda i, j: (i, 0)),
                  pl.BlockSpec((k, tn), lambda i, j: (0, j))],
        out_specs=tuple(pl.BlockSpec((tm, tn), lambda i, j: (i, j)) for _ in out_dtypes),
        compiler_params=_cparams(("arbitrary", "arbitrary"), 48),
        name=name,
    )(a, w)


def _merge_out_kernel(oa_ref, ga_ref, ogb_ref, w_ref, x_ref, gt_ref, o_ref, mixed):
    bb, tt, k = oa_ref.shape

    @pl.when(pl.program_id(2) == 0)
    def _():
        mix = _sigmoid(ga_ref[...]) * oa_ref[...] + ogb_ref[...]
        mixed[...] = mix.reshape(bb * tt, k).astype(BF16)

    y = jnp.dot(mixed[...], w_ref[...], preferred_element_type=F32)
    o_ref[...] = x_ref[...] + gt_ref[...] * y.reshape(bb, tt, -1)


def _merge_out(oa, ga, ogb, w, x, gt):
    b, t, k = oa.shape
    n = w.shape[1]
    bb, tt = _row_tiles(b, t, 512)
    tn = 512
    row = lambda i, s, j: (i, s, 0)
    return pl.pallas_call(
        _merge_out_kernel,
        out_shape=jax.ShapeDtypeStruct((b, t, n), F32),
        grid=(b // bb, t // tt, n // tn),
        in_specs=[pl.BlockSpec((bb, tt, k), row),
                  pl.BlockSpec((bb, tt, k), row),
                  pl.BlockSpec((bb, tt, k), row),
                  pl.BlockSpec((k, tn), lambda i, s, j: (0, j)),
                  pl.BlockSpec((bb, tt, tn), lambda i, s, j: (i, s, j)),
                  pl.BlockSpec((bb, 1, tn), lambda i, s, j: (i, 0, j))],
        out_specs=pl.BlockSpec((bb, tt, tn), lambda i, s, j: (i, s, j)),
        scratch_shapes=[pltpu.VMEM((bb * tt, k), BF16)],
        compiler_params=_cparams(("arbitrary", "arbitrary", "arbitrary"), 40),
        name="merge_out",
    )(oa, ga, ogb, w, x, gt)


def _ffn_kernel(h_ref, wg_ref, wu_ref, wd_ref, cw_ref, cb_ref, st_ref, y_ref, tail_ref, carry):
    s = pl.program_id(1)
    f = pl.program_id(2)
    bb, tt, d = h_ref.shape
    tm = bb * tt
    tf = wg_ref.shape[1]

    h = h_ref[...].reshape(tm, d)
    gate = jnp.dot(h, wg_ref[...], preferred_element_type=F32)
    up = jnp.dot(h, wu_ref[...], preferred_element_type=F32)

    st = st_ref[...]
    first = s == 0

    @pl.when(first)
    def _():
        carry[f] = jnp.zeros((SUBLANES, tf), F32)

    prev = carry[f]
    p0 = jnp.where(first, st[:, 0:1, :], prev[6:7, :][None])
    p1 = jnp.where(first, st[:, 1:2, :], prev[7:8, :][None])
    p0 = jnp.broadcast_to(p0, (bb, tt, tf)).reshape(tm, tf)
    p1 = jnp.broadcast_to(p1, (bb, tt, tf)).reshape(tm, tf)
    tpos = lax.broadcasted_iota(I32, (tm, tf), 0) % tt
    r1 = pltpu.roll(gate, 1, axis=0)
    r2 = pltpu.roll(gate, 2, axis=0)
    x1 = jnp.where(tpos == 0, p1, r1)
    x2 = jnp.where(tpos == 0, p0, jnp.where(tpos == 1, p1, r2))
    cw = cw_ref[...]
    a = cw[0:1] * x2 + cw[1:2] * x1 + cw[2:3] * gate + cb_ref[...]
    act = (_silu(a) * up).astype(BF16)
    contrib = jnp.dot(act, wd_ref[...], preferred_element_type=F32).reshape(bb, tt, d)

    g3 = gate.reshape(bb, tt, tf)
    tail_ref[...] = g3[:, None, tt - SUBLANES:, :]
    carry[f] = gate[tm - SUBLANES:, :]

    @pl.when(f == 0)
    def _():
        y_ref[...] = contrib

    @pl.when(f > 0)
    def _():
        y_ref[...] += contrib


def _ffn(h2, wg, wu, wd, cw, cb, st):
    b, t, d = h2.shape
    dff = wg.shape[1]
    bb, tt = _row_tiles(b, t, 1024)
    tf = 512
    nf = dff // tf
    row = lambda i, s, f: (i, s, 0)
    return pl.pallas_call(
        _ffn_kernel,
        out_shape=(jax.ShapeDtypeStruct((b, t, d), F32),
                   jax.ShapeDtypeStruct((b, t // tt, SUBLANES, dff), F32)),
        grid=(b // bb, t // tt, nf),
        in_specs=[pl.BlockSpec((bb, tt, d), row),
                  pl.BlockSpec((d, tf), lambda i, s, f: (0, f)),
                  pl.BlockSpec((d, tf), lambda i, s, f: (0, f)),
                  pl.BlockSpec((tf, d), lambda i, s, f: (f, 0)),
                  pl.BlockSpec((FFN_CONV_WIDTH, tf), lambda i, s, f: (0, f)),
                  pl.BlockSpec((1, tf), lambda i, s, f: (0, f)),
                  pl.BlockSpec((bb, FFN_CONV_WIDTH - 1, tf), lambda i, s, f: (i, 0, f))],
        out_specs=(pl.BlockSpec((bb, tt, d), row),
                   pl.BlockSpec((bb, 1, SUBLANES, tf), lambda i, s, f: (i, s, 0, f))),
        scratch_shapes=[pltpu.VMEM((nf, SUBLANES, tf), F32)],
        compiler_params=_cparams(("arbitrary", "arbitrary", "arbitrary"), 56),
        name="conv_ffn",
    )(h2, wg, wu, wd, cw, cb, st)


def _res_norm_kernel(x_ref, y_ref, gt_ref, gf_ref, o_ref):
    x = x_ref[...] + gt_ref[...] * y_ref[...]
    o_ref[...] = x * lax.rsqrt(jnp.mean(x * x, axis=-1, keepdims=True) + EPS) * gf_ref[...]


def _res_norm(x1, y, gt, gf):
    b, t, d = x1.shape
    bb, tt = _row_tiles(b, t, 512)
    row = lambda i, j: (i, j, 0)
    return pl.pallas_call(
        _res_norm_kernel,
        out_shape=jax.ShapeDtypeStruct((b, t, d), F32),
        grid=(b // bb, t // tt),
        in_specs=[pl.BlockSpec((bb, tt, d), row),
                  pl.BlockSpec((bb, tt, d), row),
                  pl.BlockSpec((bb, 1, d), lambda i, j: (i, 0, 0)),
                  pl.BlockSpec((1, d), lambda i, j: (0, 0))],
        out_specs=pl.BlockSpec((bb, tt, d), row),
        compiler_params=_cparams(("arbitrary", "arbitrary"), 40),
        name="res_norm",
    )(x1, y, gt, gf)


HEADS_PER_GROUP = 4
GROUP_ROWS = HEADS_PER_GROUP * CHUNK
_DELTA_PASSES = dict(kk=1, inv=1, uw=1, qk=1, state=1, intra=1, upd=1)


def _delta_kernel(x_ref, st_ref, cw_ref, ab_ref, hp_ref, s0_ref, z_ref, gb_ref, gdn_ref,
                  o_ref, s_ref, carry, *, nheads):
    c = pl.program_id(1)
    r = GROUP_ROWS
    hd = DK_B

    @pl.when(c == 0)
    def _():
        carry[...] = st_ref[0]
        s_ref[...] = s0_ref[...]

    prev = carry[...]
    carry[...] = x_ref[0, CHUNK - SUBLANES:, :]

    ri = lax.broadcasted_iota(I32, (r, r), 0)
    ci = lax.broadcasted_iota(I32, (r, r), 1)
    same_head = (ri // CHUNK) == (ci // CHUNK)
    incl = same_head & (ri >= ci)
    strict = same_head & (ri > ci)
    triu = jnp.where(same_head & (ri <= ci), 1.0, 0.0).astype(F32)
    eye = jnp.where(ri == ci, 1.0, 0.0).astype(F32)
    row8 = lax.broadcasted_iota(I32, (SUBLANES, r * 2), 0)
    rid = lax.broadcasted_iota(I32, (LANES, r), 0)

    def conv_silu(col0, width):
        xs = x_ref[0, :, col0:col0 + width]
        pv = prev[:, col0:col0 + width]
        w = cw_ref[:, col0:col0 + width]
        y = xs * w[CONVB_WIDTH - 1:CONVB_WIDTH]
        for k in range(1, CONVB_WIDTH):
            rolled = pltpu.roll(xs, k, axis=0)
            head = jnp.where(row8[:, :width] < k, pltpu.roll(pv, k, axis=0), rolled[:SUBLANES])
            shifted = jnp.concatenate([head, rolled[SUBLANES:]], axis=0)
            y = y + shifted * w[CONVB_WIDTH - 1 - k:CONVB_WIDTH - k]
        return _silu(y)

    def l2n(v):
        return v * lax.rsqrt(jnp.sum(v * v, axis=-1, keepdims=True) + EPS)

    def stack(y, fn):
        return jnp.concatenate([fn(y[:, j * hd:(j + 1) * hd]) for j in range(HEADS_PER_GROUP)], axis=0)

    ngroups = nheads // HEADS_PER_GROUP
    gw = HEADS_PER_GROUP * hd
    pp = _DELTA_PASSES

    def prep(g):
        qs = stack(conv_silu(g * gw, gw), lambda v: l2n(v) * (DK_B ** -0.5))
        ks = stack(conv_silu(nheads * hd + g * gw, gw), l2n)
        vs = stack(conv_silu(2 * nheads * hd + g * gw, gw), lambda v: v)
        a_row = ab_ref[0, 0, 0:1, g * r:(g + 1) * r]
        b_row = ab_ref[0, 0, 1:2, g * r:(g + 1) * r]
        alog = hp_ref[0:1, g * r:(g + 1) * r]
        dtb = hp_ref[1:2, g * r:(g + 1) * r]
        xsp = a_row + dtb
        softplus = jnp.maximum(xsp, 0.0) + jnp.log1p(jnp.exp(-jnp.abs(xsp)))
        g_row = -jnp.exp(alog) * softplus
        beta_row = _sigmoid(b_row)
        gc_row = _dot(jnp.broadcast_to(g_row, (SUBLANES, r)), triu, passes=6)[0:1]
        rows = jnp.where(rid == 0, beta_row, jnp.where(rid == 1, gc_row, 0.0))
        cols = rows.T
        beta_col = cols[:, 0:1]
        gc_col = cols[:, 1:2]
        gl_col = jnp.concatenate(
            [jnp.broadcast_to(cols[(j + 1) * CHUNK - 1:(j + 1) * CHUNK, 1:2], (CHUNK, 1))
             for j in range(HEADS_PER_GROUP)], axis=0)
        decay = jnp.where(incl, jnp.exp(gc_col - gc_row), 0.0)
        egc = jnp.exp(gc_col)
        kb = ks * beta_col
        return dict(qs=qs, ks=ks, kb=kb, rhs=jnp.concatenate([vs * beta_col, kb * egc], axis=1),
                    decay=decay, qg=qs * egc, kd=ks * jnp.exp(gl_col - gc_col), gl_col=gl_col)

    gs = [prep(g) for g in range(ngroups)]
    ms = [jnp.where(strict, _dot(d["kb"], d["ks"], _NT, passes=pp["kk"]) * d["decay"], 0.0) for d in gs]

    invs = [eye - jnp.where((ri // 2 == ci // 2), m, 0.0) for m in ms]
    bs = 2
    while bs < CHUNK:
        lower = ((ri // (2 * bs)) == (ci // (2 * bs))) & ((ri % (2 * bs)) >= bs) & ((ci % (2 * bs)) < bs)
        xc = [_dot(inv, jnp.where(lower, m, 0.0), passes=pp["inv"]) for inv, m in zip(invs, ms)]
        invs = [inv - _dot(t, inv, passes=pp["inv"]) for inv, t in zip(invs, xc)]
        bs *= 2

    uws = [_dot(inv, d["rhs"], passes=pp["uw"]) for inv, d in zip(invs, gs)]
    attns = [jnp.where(incl, _dot(d["qs"], d["ks"], _NT, passes=pp["qk"]) * d["decay"], 0.0) for d in gs]

    vnews, qstates = [], []
    for g in range(ngroups):
        vnew, qstate = [], []
        for j in range(HEADS_PER_GROUP):
            sl = slice(j * CHUNK, (j + 1) * CHUNK)
            wq = jnp.concatenate([uws[g][sl, hd:], gs[g]["qg"][sl]], axis=0)
            rs = _dot(wq, s_ref[0, g * HEADS_PER_GROUP + j], passes=pp["state"])
            vnew.append(uws[g][sl, :hd] - rs[:CHUNK])
            qstate.append(rs[CHUNK:])
        vnews.append(vnew)
        qstates.append(qstate)
    intras = [_dot(attns[g], jnp.concatenate(vnews[g], axis=0), passes=pp["intra"]) for g in range(ngroups)]

    for g in range(ngroups):
        for j in range(HEADS_PER_GROUP):
            hidx = g * HEADS_PER_GROUP + j
            sl = slice(j * CHUNK, (j + 1) * CHUNK)
            hs = slice(hidx * hd, (hidx + 1) * hd)
            o = qstates[g][j] + intras[g][sl]
            glast = jnp.exp(gs[g]["gl_col"][j * CHUNK:j * CHUNK + 1, :])
            upd = _dot(gs[g]["kd"][sl], vnews[g][j], _TN, passes=pp["upd"])
            s_ref[0, hidx] = s_ref[0, hidx] * glast + upd
            on = o * lax.rsqrt(jnp.mean(o * o, axis=-1, keepdims=True) + EPS) * gdn_ref[...]
            o_ref[0, :, hs] = on * _silu(z_ref[0, :, hs]) * _sigmoid(gb_ref[0, :, hs])


def _delta(xqkv, st8, cw, ab, hp, s0, z, gb, gdn):
    b, t, cdim = xqkv.shape
    nheads = s0.shape[1]
    dm = z.shape[2]
    n = t // CHUNK
    kern = functools.partial(_delta_kernel, nheads=nheads)
    return pl.pallas_call(
        kern,
        out_shape=(jax.ShapeDtypeStruct((b, t, dm), F32),
                   jax.ShapeDtypeStruct(s0.shape, F32)),
        grid=(b, n),
        in_specs=[pl.BlockSpec((1, CHUNK, cdim), lambda i, c: (i, c, 0)),
                  pl.BlockSpec((1, SUBLANES, cdim), lambda i, c: (i, 0, 0)),
                  pl.BlockSpec((CONVB_WIDTH, cdim), lambda i, c: (0, 0)),
                  pl.BlockSpec((1, 1, SUBLANES, nheads * CHUNK), lambda i, c: (i, c, 0, 0)),
                  pl.BlockSpec((SUBLANES, nheads * CHUNK), lambda i, c: (0, 0)),
                  pl.BlockSpec((1, nheads, DK_B, DV_B), lambda i, c: (i, 0, 0, 0)),
                  pl.BlockSpec((1, CHUNK, dm), lambda i, c: (i, c, 0)),
                  pl.BlockSpec((1, CHUNK, dm), lambda i, c: (i, c, 0)),
                  pl.BlockSpec((1, DV_B), lambda i, c: (0, 0))],
        out_specs=(pl.BlockSpec((1, CHUNK, dm), lambda i, c: (i, c, 0)),
                   pl.BlockSpec((1, nheads, DK_B, DV_B), lambda i, c: (i, 0, 0, 0))),
        scratch_shapes=[pltpu.VMEM((SUBLANES, cdim), F32)],
        compiler_params=_cparams(("arbitrary", "arbitrary"), 40),
        name="gated_delta",
    )(xqkv, st8, cw, ab, hp, s0, z, gb, gdn)


_T5_THRESHOLDS = tuple(math.ceil(8 * 2 ** (k / 2)) for k in range(1, 8))
_FAR_BUCKET = NUM_BUCKETS // 2 - 1


MASKED = -1e30
TOPK_CANDIDATES = 12
KEY_TILE = 256
FLASH_ROWS = 512


def _topk_kernel(qi_ref, sm_ref, kit_ref, mask_ref, skey, cand, thr, jcut, qis, wib,
                 *, past, qb, tk, rb, ncand, n_sel, nkt):
    i = pl.program_id(1)
    rep = tk // LANES
    kd = past // tk + i * (qb // tk)
    q0 = past + i * qb

    def sortable(x):
        bits = pltpu.bitcast(x, I32)
        return bits ^ ((bits >> 31) & 0x7FFFFFFF)

    def lanes(v):
        return jnp.concatenate([v] * rep, axis=1) if rep > 1 else v

    if True:
        sm = sm_ref[0]
        for h in range(H_I):
            qis[h] = qi_ref[0, :, h * D_IDX:(h + 1) * D_IDX]
            wcol = sm[:, D_IDX + h:D_IDX + h + 1] * ((H_I * D_IDX) ** -0.5)
            wib[h] = jnp.broadcast_to(wcol, (qb, LANES))

        qpos = q0 + lax.broadcasted_iota(I32, (qb, tk), 0)
        lim = (qpos // CHUNK + 1) * CHUNK
        lane = lax.broadcasted_iota(I32, (qb, tk), 1)

        def score_tile(j, _):
            off = pl.multiple_of(j * tk, tk)
            kit = kit_ref[0, :, pl.ds(off, tk)]
            sc = jnp.zeros((qb, tk), F32)
            for h in range(H_I):
                d = jnp.dot(qis[h], kit, preferred_element_type=F32)
                sc = sc + jnp.maximum(d, 0.0) * lanes(wib[h])
            sc = sc + 0.0
            ok = lane + off < lim
            skey[:, pl.ds(off, tk)] = jnp.where(ok, sortable(sc), INT_MIN)
            sc = jnp.where(ok, sc, -jnp.inf)
            for u in range(rep):
                x = sc[:, u * LANES:(u + 1) * LANES]
                for c in range(ncand):
                    a = cand[c]
                    cand[c] = jnp.maximum(a, x)
                    x = jnp.minimum(a, x)
            return 0

        cand[...] = jnp.full(cand.shape, -jnp.inf, F32)
        lax.fori_loop(0, kd + 1, score_tile, 0)
        for c in range(ncand):
            a = cand[c]
            cand[c] = pltpu.bitcast(jnp.where(a == -jnp.inf, INT_MIN, sortable(a)), F32)

        def kth_largest(count_ge, rows):
            def bit_body(bi, t):
                cnd = t + lax.shift_left(jnp.int32(1), jnp.int32(31) - bi)
                return jnp.where(count_ge(cnd) >= n_sel, cnd, t)
            return lax.fori_loop(0, 32, bit_body, jnp.full((rows, LANES), INT_MIN, I32))

        def row_total(c):
            return jnp.broadcast_to(jnp.sum(c, axis=1, keepdims=True), c.shape)

        def cand_count_ge(cnd):
            c = jnp.zeros((qb, LANES), I32)
            for k in range(ncand):
                c = c + jnp.where(pltpu.bitcast(cand[k], I32) >= cnd, 1, 0)
            return row_total(c)

        thr[...] = kth_largest(cand_count_ge, qb)

        def finish_rows(rblk, bad, full_search):
            r0 = pl.multiple_of(rblk * rb, rb)
            rlane = lax.broadcasted_iota(I32, (rb, tk), 1)

            def count(pred):
                def body(jj, c):
                    off = pl.multiple_of(jj * tk, tk)
                    mk = jnp.where(pred(skey[pl.ds(r0, rb), pl.ds(off, tk)], off), 1, 0)
                    part = mk[:, :LANES]
                    for u in range(1, rep):
                        part = part + mk[:, u * LANES:(u + 1) * LANES]
                    return c + part
                return row_total(lax.fori_loop(0, kd + 1, body, jnp.zeros((rb, LANES), I32)))

            if full_search:
                t = kth_largest(lambda cnd: count(lambda x, off: x >= lanes(cnd)), rb)
            else:
                t = thr[pl.ds(r0, rb), :]
            ts = jnp.maximum(t, INT_MIN + 1)
            thr[pl.ds(r0, rb), :] = ts
            jcut[pl.ds(r0, rb), :] = jnp.full((rb, LANES), 2 ** 30, I32)

            tsl = lanes(ts)
            c_gt = count(lambda x, off: x > tsl)
            c_eq = count(lambda x, off: x == tsl)
            need = n_sel - c_gt
            excess = jnp.max(jnp.where((c_eq > need) & (t != INT_MIN), 1.0, 0.0))

            @pl.when(excess > 0.5)
            def _():
                def jbit(bi, jc):
                    cnd = jc | lax.shift_left(jnp.int32(1), jnp.int32(15) - bi)
                    cl = lanes(cnd)
                    cnt = count(lambda x, off: (x == tsl) & (rlane + off < cl))
                    return jnp.where(cnt <= need, cnd, jc)
                jcut[pl.ds(r0, rb), :] = lax.fori_loop(0, 16, jbit, jnp.zeros((rb, LANES), I32))

            return jnp.maximum(bad, jnp.max(jnp.where(need <= 0, 1.0, 0.0)))

        bad = lax.fori_loop(0, qb // rb, functools.partial(finish_rows, full_search=False), jnp.float32(0.0))

        @pl.when(bad > 0.5)
        def _():
            lax.fori_loop(0, qb // rb, functools.partial(finish_rows, full_search=True), jnp.float32(0.0))

    ts = lanes(thr[...])
    jc = lanes(jcut[...])
    lane = lax.broadcasted_iota(I32, (qb, tk), 1)

    def mask_tile(jj, _):
        off = pl.multiple_of(jj * tk, tk)
        x = skey[:, pl.ds(off, tk)]
        sel = (x > ts) | ((x == ts) & (lane + off < jc))
        mask_ref[0, :, pl.ds(off, tk)] = jnp.where(sel, 0.0, MASKED).astype(mask_ref.dtype)
        return 0

    def blank_tile(jj, _):
        off = pl.multiple_of(jj * tk, tk)
        mask_ref[0, :, pl.ds(off, tk)] = jnp.full((qb, tk), MASKED, mask_ref.dtype)
        return 0

    lax.fori_loop(0, kd + 1, mask_tile, 0)
    lax.fori_loop(kd + 1, nkt, blank_tile, 0)


def _topk_mask(qi, small, kit, *, past, n_sel):
    b, t, _ = qi.shape
    lp = kit.shape[2]
    tk = KEY_TILE
    qb = min(t, tk)
    assert past % tk == 0 and lp % tk == 0 and t % qb == 0
    assert qb == tk or t == qb
    assert lp < 2 ** 16
    kern = functools.partial(_topk_kernel, past=past, qb=qb, tk=tk, rb=min(qb, 128), ncand=TOPK_CANDIDATES,
                             n_sel=n_sel, nkt=lp // tk)
    row = lambda bi, i: (bi, i, 0)
    return pl.pallas_call(
        kern,
        out_shape=jax.ShapeDtypeStruct((b, t, lp), BF16),
        grid=(b, t // qb),
        in_specs=[pl.BlockSpec((1, qb, H_I * D_IDX), row),
                  pl.BlockSpec((1, qb, LANES), row),
                  pl.BlockSpec((1, D_IDX, lp), lambda bi, i: (bi, 0, 0), pipeline_mode=pl.Buffered(1))],
        out_specs=pl.BlockSpec((1, qb, lp), row),
        scratch_shapes=[pltpu.VMEM((qb, lp), I32),
                        pltpu.VMEM((TOPK_CANDIDATES, qb, LANES), F32),
                        pltpu.VMEM((qb, LANES), I32),
                        pltpu.VMEM((qb, LANES), I32),
                        pltpu.VMEM((H_I, qb, D_IDX), BF16),
                        pltpu.VMEM((H_I, qb, LANES), F32)],
        compiler_params=_cparams(("arbitrary", "arbitrary"), 56),
        name="topk_mask",
    )(qi, small, kit)


def _flash_kernel(itab, jtab, rb_ref, q_ref, ma_ref, mb_ref, ka_ref, kb_ref, va_ref, vb_ref, o_ref,
                  madd, m_scr, l_scr, acc, dtile, *, past, qf, tk, nheads):
    b = pl.program_id(0)
    step = pl.program_id(1)
    i = itab[step]
    j = jtab[step]
    hd = HD_A
    rs = min(qf, tk)
    nsub = qf // rs
    kd_last = past // tk + (i * qf + qf - 1) // tk
    blk_b = kd_last - 2 * j
    blk_a = blk_b - 1

    @pl.when((b == 0) & (step == 0))
    def _():
        rr = lax.broadcasted_iota(I32, (rs, tk), 0)
        cc = lax.broadcasted_iota(I32, (rs, tk), 1)
        for slot in (0, 1):
            rel = cc - rr - slot * tk
            n = jnp.abs(rel)
            large = jnp.full((rs, tk), NUM_BUCKETS // 4, I32)
            for th in _T5_THRESHOLDS:
                large = large + jnp.where(n >= th, 1, 0)
            bucket = jnp.where(rel > 0, NUM_BUCKETS // 2, 0) + jnp.where(n < NUM_BUCKETS // 4, n, large)

            def head_body(h, _, bucket=bucket, slot=slot):
                base = rb_ref[_FAR_BUCKET, h]
                tile = jnp.zeros((rs, tk), F32)
                for bk in range(NUM_BUCKETS):
                    if bk != _FAR_BUCKET:
                        tile = jnp.where(bucket == bk, (rb_ref[bk, h] - base) * LOG2E, tile)
                dtile[h, slot] = tile
                return 0

            lax.fori_loop(0, nheads, head_body, 0)

    @pl.when(j == 0)
    def _():
        m_scr[...] = jnp.full(m_scr.shape, MASKED, F32)
        l_scr[...] = jnp.zeros(l_scr.shape, F32)
        acc[...] = jnp.zeros(acc.shape, F32)

    hang = jnp.where(blk_a < 0, MASKED, 0.0)
    madd[:, :tk] = ma_ref[0].astype(F32) + hang
    madd[:, tk:] = mb_ref[0].astype(F32)

    def bias(h, sub, blk):
        back = kd_last - (nsub - 1 - sub) - blk
        on = jnp.where((back == 0) | (back == 1), 1.0, 0.0)
        return dtile[h, jnp.clip(back, 0, 1)] * on

    def heads(near):
        for h in range(nheads):
            head(h, near)

    def head(h, near):
        hs = slice(h * hd, (h + 1) * hd)
        qh = q_ref[0, :, hs]
        sa = lax.dot_general(qh, ka_ref[0, :, hs], _NT, preferred_element_type=F32)
        sb = lax.dot_general(qh, kb_ref[0, :, hs], _NT, preferred_element_type=F32)
        if near:
            rows = []
            for sub in range(nsub):
                r = slice(sub * rs, (sub + 1) * rs)
                rows.append(jnp.concatenate([sa[r] + bias(h, sub, blk_a), sb[r] + bias(h, sub, blk_b)], axis=1))
            s = jnp.concatenate(rows, axis=0) if nsub > 1 else rows[0]
        else:
            s = jnp.concatenate([sa, sb], axis=1)
        s = s + madd[...]
        m_prev = m_scr[h]
        m_new = jnp.maximum(m_prev, jnp.max(s, axis=1, keepdims=True))
        alpha = jnp.exp2(m_prev - m_new)
        p = jnp.exp2(s - jnp.concatenate([m_new] * (2 * tk // LANES), axis=1))
        l_scr[h] = alpha * l_scr[h] + jnp.sum(p, axis=1, keepdims=True)
        m_scr[h] = m_new
        pb = p.astype(BF16)
        pv = (jnp.dot(pb[:, :tk], va_ref[0, :, hs], preferred_element_type=F32)
              + jnp.dot(pb[:, tk:], vb_ref[0, :, hs], preferred_element_type=F32))
        acc[:, hs] = alpha * acc[:, hs] + pv

    near_steps = nsub // 2 + 1

    @pl.when(j < near_steps)
    def _():
        heads(True)

    @pl.when(j >= near_steps)
    def _():
        heads(False)

    @pl.when(j == kd_last // 2)
    def _():
        for h in range(nheads):
            hs = slice(h * hd, (h + 1) * hd)
            o_ref[0, :, hs] = (acc[:, hs] / l_scr[h]).astype(o_ref.dtype)


def _flash(rel_bias, q, mask, kall, vall, *, past):
    b, t, dm = q.shape
    lp = kall.shape[1]
    nheads = dm // HD_A
    tk = KEY_TILE
    qf = min(t, FLASH_ROWS)
    assert past % tk == 0 and lp % tk == 0 and t % qf == 0
    assert qf % tk == 0 or t == qf
    klast = lambda i: past // tk + (i * qf + qf - 1) // tk
    steps = [(i, j) for i in range(t // qf) for j in range(klast(i) // 2 + 1)]
    itab = jnp.asarray([s[0] for s in steps], I32)
    jtab = jnp.asarray([s[1] for s in steps], I32)
    kern = functools.partial(_flash_kernel, past=past, qf=qf, tk=tk, nheads=nheads)
    rs = min(qf, tk)

    def row(bi, s, it, jt):
        return (bi, it[s], 0)

    def tile_a(bi, s, it, jt):
        return (bi, jnp.maximum(klast(it[s]) - 2 * jt[s] - 1, 0), 0)

    def tile_b(bi, s, it, jt):
        return (bi, klast(it[s]) - 2 * jt[s], 0)

    def mask_a(bi, s, it, jt):
        return (bi, it[s], jnp.maximum(klast(it[s]) - 2 * jt[s] - 1, 0))

    def mask_b(bi, s, it, jt):
        return (bi, it[s], klast(it[s]) - 2 * jt[s])

    grid_spec = pltpu.PrefetchScalarGridSpec(
        num_scalar_prefetch=2,
        grid=(b, len(steps)),
        in_specs=[pl.BlockSpec(memory_space=pltpu.SMEM),
                  pl.BlockSpec((1, qf, dm), row),
                  pl.BlockSpec((1, qf, tk), mask_a),
                  pl.BlockSpec((1, qf, tk), mask_b),
                  pl.BlockSpec((1, tk, dm), tile_a),
                  pl.BlockSpec((1, tk, dm), tile_b),
                  pl.BlockSpec((1, tk, dm), tile_a),
                  pl.BlockSpec((1, tk, dm), tile_b)],
        out_specs=pl.BlockSpec((1, qf, dm), row),
        scratch_shapes=[pltpu.VMEM((qf, 2 * tk), F32),
                        pltpu.VMEM((nheads, qf, LANES), F32),
                        pltpu.VMEM((nheads, qf, LANES), F32),
                        pltpu.VMEM((qf, dm), F32),
                        pltpu.VMEM((nheads, 2, rs, tk), F32)])
    return pl.pallas_call(
        kern,
        out_shape=jax.ShapeDtypeStruct((b, t, dm), BF16),
        grid_spec=grid_spec,
        compiler_params=_cparams(("arbitrary", "arbitrary"), 56),
        name="masked_flash",
    )(itab, jtab, rel_bias, q, mask, mask, kall, kall, vall, vall)


def _pad_rows(a, rows):
    return jnp.pad(a, [(0, 0), (rows - a.shape[1], 0)] + [(0, 0)] * (a.ndim - 2))


def _run_group(x, mod, cache_k, cache_v, cache_kidx, st_conv, st_delta, st_ffn, rel_bias, g_final, w):
    b, t, d = x.shape
    nheads = d // HD_A
    sh1, sc1, gt1, sh2, sc2, gt2 = [m[:, None, :] for m in jnp.split(mod, 6, axis=-1)]
    m_rows = b * t

    h = _norm_mod(x, w["g_norm1"], sc1, sh1).reshape(m_rows, d)
    def proj(name, *dts):
        outs = _matmul(h, w[name], dts or (F32,), name="proj_" + name)
        outs = [o.reshape(b, t, -1) for o in outs]
        return outs if len(outs) > 1 else outs[0]

    qa, qi = proj("w_qa", BF16), proj("w_qi", BF16)
    (ka, ka16), (va, va16) = proj("w_ka", F32, BF16), proj("w_va", F32, BF16)
    small = proj("w_small")
    qkv_b, z_b, gate_a, gate_b = proj("w_qkvb"), proj("w_z"), proj("w_ga"), proj("w_gb")
    ki = small[..., :D_IDX]

    past = 0 if cache_k is None else cache_k.shape[1]
    l_all = past + t
    tk = KEY_TILE
    lp = -(-l_all // tk) * tk
    if cache_k is None:
        k_all, v_all, ki_all = ka16, va16, ki
    else:
        k_all = jnp.concatenate([cache_k.reshape(b, past, d).astype(BF16), ka16], axis=1)
        v_all = jnp.concatenate([cache_v.reshape(b, past, d).astype(BF16), va16], axis=1)
        ki_all = jnp.concatenate([cache_kidx, ki], axis=1)
    padk = lambda a: jnp.pad(a, ((0, 0), (0, lp - l_all), (0, 0))).astype(BF16)
    k_all, v_all = padk(k_all), padk(v_all)
    kit = jnp.swapaxes(padk(ki_all), 1, 2)
    n_sel = min(TOPK_MAX, l_all // 4)
    mask = _topk_mask(qi, small, kit, past=past, n_sel=n_sel)
    o_a = _flash(rel_bias, qa, mask, k_all, v_all, past=past)

    n = t // CHUNK
    to_rows = lambda v: jnp.swapaxes(v.reshape(b, n, CHUNK, nheads), 2, 3).reshape(b, n, 1, nheads * CHUNK)
    ab = jnp.concatenate([to_rows(small[..., 80:96]), to_rows(small[..., 96:112]),
                          jnp.zeros((b, n, SUBLANES - 2, nheads * CHUNK), F32)], axis=2)
    ogb, new_delta = _delta(qkv_b, _pad_rows(st_conv, SUBLANES), w["conv_b_w"], ab, w["hp_rows"],
                            st_delta, z_b, gate_b, w["g_delta_norm"])
    assert t >= CONVB_WIDTH - 1
    new_conv = qkv_b[:, t - (CONVB_WIDTH - 1):]

    x1 = _merge_out(o_a, gate_a, ogb, w["w_out"], x, gt1)
    h2 = _norm_mod(x1, w["g_norm2"], sc2, sh2)
    ffn, tail = _ffn(h2, w["w_ffn_gate"], w["w_ffn_up"], w["w_ffn_down"], w["ffn_conv_w"], w["ffn_conv_b"], st_ffn)
    y = _res_norm(x1, ffn, gt2, g_final)
    new_ffn = tail[:, -1, SUBLANES - (FFN_CONV_WIDTH - 1):]
    kshape = (1, b, t, nheads, HD_A)
    return (y, ka.reshape(kshape), va.reshape(kshape), ki[None], new_conv[None], new_delta[None], new_ffn[None])


def kernel(x_prompt, x_sample, c_prompt, c_sample, cache_k, cache_v, cache_kidx, state_conv_b, state_delta,
           state_ffn_conv, rel_bias, g_final, w_ada, b_ada, g_norm1, w_in, conv_b_w, a_log, dt_bias,
           g_delta_norm, w_out, g_norm2, w_ffn_gate, ffn_conv_w, ffn_conv_b, w_ffn_up, w_ffn_down):
    assert w_ada.shape[0] == 1, "one layer"
    d = x_prompt.shape[-1]
    nheads = d // HD_A
    bp, bs = x_prompt.shape[0], x_sample.shape[0]
    convb = 3 * nheads * DK_B

    wi_ = w_in[0]
    sizes = (d, d, d, H_I * D_IDX, D_IDX, H_I, convb, nheads, nheads, d, d, d)
    offs = [0]
    for s in sizes:
        offs.append(offs[-1] + s)
    col = lambda k: wi_[:, offs[k]:offs[k + 1]]
    w_small = jnp.concatenate([col(4), col(5), col(7), col(8),
                               jnp.zeros((d, LANES - D_IDX - H_I - 2 * nheads), F32)], axis=1)
    bf = lambda a: a.astype(BF16)
    w = dict(
        w_qa=bf(col(0) * (HD_A ** -0.5 * LOG2E)), w_ka=bf(col(1)), w_va=bf(col(2)), w_qi=bf(col(3)), w_small=bf(w_small),
        w_qkvb=bf(col(6)), w_z=bf(col(9)), w_ga=bf(col(10)), w_gb=bf(col(11)),
        g_norm1=g_norm1, g_norm2=g_norm2, conv_b_w=conv_b_w[0], g_delta_norm=g_delta_norm,
        w_out=bf(w_out[0]), w_ffn_gate=bf(w_ffn_gate[0]), w_ffn_up=bf(w_ffn_up[0]),
        w_ffn_down=bf(w_ffn_down[0]), ffn_conv_w=ffn_conv_w[0], ffn_conv_b=ffn_conv_b,
        hp_rows=jnp.concatenate([jnp.repeat(a_log[0], CHUNK)[None], jnp.repeat(dt_bias[0], CHUNK)[None],
                                 jnp.zeros((SUBLANES - 2, nheads * CHUNK), F32)], axis=0),
    )

    c_all = jnp.concatenate([c_prompt, c_sample], axis=0)
    rows = -(-c_all.shape[0] // SUBLANES) * SUBLANES
    c_all = jnp.pad(c_all, ((0, rows - c_all.shape[0]), (0, 0)))
    mod = _adaln(c_all, w_ada[0], b_ada)

    zeros = lambda *s: jnp.zeros(s, F32)
    outs_p = _run_group(x_prompt, mod[:bp], None, None, None,
                        zeros(bp, CONVB_WIDTH - 1, convb), zeros(bp, nheads, DK_B, DV_B),
                        zeros(bp, FFN_CONV_WIDTH - 1, w_ffn_gate.shape[-1]), rel_bias, g_final[None], w)
    outs_s = _run_group(x_sample, mod[bp:bp + bs], cache_k[0], cache_v[0], cache_kidx[0],
                        state_conv_b[0], state_delta[0], state_ffn_conv[0], rel_bias, g_final[None], w)
    return (outs_p[0], outs_s[0]) + tuple(outs_p[1:]) + tuple(outs_s[1:])
```

```python
import functools
import math

import jax
import jax.numpy as jnp
from jax import lax
from jax.experimental import pallas as pl
from jax.experimental.pallas import tpu as pltpu

F32 = jnp.float32
BF16 = jnp.bfloat16
I32 = jnp.int32

CHUNK = 64
HD_A = 128
H_I = 16
D_IDX = 64
TOPK_MAX = 256
NUM_BUCKETS = 32
MAX_DISTANCE = 128
DK_B = 128
DV_B = 128
CONVB_WIDTH = 4
FFN_CONV_WIDTH = 3
EPS = 1e-6
LOG2E = math.log2(math.e)

V7X_VMEM_BYTES = 64 * 1024 * 1024
LANES = 128
SUBLANES = 8
INT_MIN = -(2 ** 31)

MATMUL_ROWS, MATMUL_COLS = 2048, 512
FFN_ROWS, FFN_COLS = 1024, 512
MERGE_ROWS = 512
ELEMENTWISE_ROWS = 512
ADALN_COLS = 1024


def _cparams(sem, vmem_mb):
    assert vmem_mb * 1024 * 1024 < V7X_VMEM_BYTES
    return pltpu.CompilerParams(dimension_semantics=sem, vmem_limit_bytes=vmem_mb * 1024 * 1024)


def _sigmoid(x):
    return 1.0 / (1.0 + jnp.exp(-x))


def _silu(x):
    return x * _sigmoid(x)


def _dot(a, b, dims=None, exact=False):
    dn = dims if dims is not None else (((a.ndim - 1,), (0,)), ((), ()))
    if exact:
        return lax.dot_general(a, b, dn, precision=lax.Precision.HIGHEST, preferred_element_type=F32)
    return lax.dot_general(a.astype(BF16), b.astype(BF16), dn, preferred_element_type=F32)


_NT = (((1,), (1,)), ((), ()))
_TN = (((0,), (0,)), ((), ()))


def _adaln_kernel(c_ref, w_ref, b_ref, o_ref):
    o_ref[...] = _dot(_silu(c_ref[...]), w_ref[...]) + b_ref[...]


def _adaln(c, w, b):
    bp, d = c.shape
    n = w.shape[1]
    tn = ADALN_COLS
    return pl.pallas_call(
        _adaln_kernel,
        out_shape=jax.ShapeDtypeStruct((bp, n), F32),
        grid=(n // tn,),
        in_specs=[pl.BlockSpec((bp, d), lambda j: (0, 0)),
                  pl.BlockSpec((d, tn), lambda j: (0, j)),
                  pl.BlockSpec((1, tn), lambda j: (0, j))],
        out_specs=pl.BlockSpec((bp, tn), lambda j: (0, j)),
        compiler_params=_cparams(("arbitrary",), 40),
        name="adaln",
    )(c, w, b)


def _norm_mod_kernel(x_ref, g_ref, sc_ref, sh_ref, o_ref):
    x = x_ref[...]
    y = x * lax.rsqrt(jnp.mean(x * x, axis=-1, keepdims=True) + EPS)
    o_ref[...] = ((y * g_ref[...]) * (1.0 + sc_ref[...]) + sh_ref[...]).astype(o_ref.dtype)


def _row_tiles(b, t, rows):
    if t >= rows:
        return 1, rows
    bb = max(1, min(b, rows // t))
    while b % bb:
        bb -= 1
    return bb, t


def _norm_mod(x, g, sc, sh):
    b, t, d = x.shape
    bb, tt = _row_tiles(b, t, ELEMENTWISE_ROWS)
    return pl.pallas_call(
        _norm_mod_kernel,
        out_shape=jax.ShapeDtypeStruct((b, t, d), BF16),
        grid=(b // bb, t // tt),
        in_specs=[pl.BlockSpec((bb, tt, d), lambda i, j: (i, j, 0)),
                  pl.BlockSpec((1, d), lambda i, j: (0, 0)),
                  pl.BlockSpec((bb, 1, d), lambda i, j: (i, 0, 0)),
                  pl.BlockSpec((bb, 1, d), lambda i, j: (i, 0, 0))],
        out_specs=pl.BlockSpec((bb, tt, d), lambda i, j: (i, j, 0)),
        compiler_params=_cparams(("arbitrary", "arbitrary"), 32),
        name="norm_mod",
    )(x, g, sc, sh)


def _mm_kernel(a_ref, w_ref, *o_refs):
    acc = jnp.dot(a_ref[...], w_ref[...], preferred_element_type=F32)
    for o_ref in o_refs:
        o_ref[...] = acc.astype(o_ref.dtype)


def _matmul(a, w, out_dtypes=(F32,), name="proj"):
    m, k = a.shape
    n = w.shape[1]
    tm = min(m, MATMUL_ROWS)
    tn = min(n, MATMUL_COLS)
    return pl.pallas_call(
        _mm_kernel,
        out_shape=tuple(jax.ShapeDtypeStruct((m, n), dt) for dt in out_dtypes),
        grid=(m // tm, n // tn),
        in_specs=[pl.BlockSpec((tm, k), lambda i, j: (i, 0)),
                  pl.BlockSpec((k, tn), lambda i, j: (0, j))],
        out_specs=tuple(pl.BlockSpec((tm, tn), lambda i, j: (i, j)) for _ in out_dtypes),
        compiler_params=_cparams(("arbitrary", "arbitrary"), 48),
        name=name,
    )(a, w)


def _merge_out_kernel(oa_ref, ga_ref, ogb_ref, w_ref, x_ref, gt_ref, o_ref, mixed):
    bb, tt, k = oa_ref.shape

    @pl.when(pl.program_id(2) == 0)
    def _():
        mix = _sigmoid(ga_ref[...]) * oa_ref[...] + ogb_ref[...]
        mixed[...] = mix.reshape(bb * tt, k).astype(BF16)

    y = jnp.dot(mixed[...], w_ref[...], preferred_element_type=F32)
    o_ref[...] = x_ref[...] + gt_ref[...] * y.reshape(bb, tt, -1)


def _merge_out(oa, ga, ogb, w, x, gt):
    b, t, k = oa.shape
    n = w.shape[1]
    bb, tt = _row_tiles(b, t, MERGE_ROWS)
    tn = MATMUL_COLS
    row = lambda i, s, j: (i, s, 0)
    return pl.pallas_call(
        _merge_out_kernel,
        out_shape=jax.ShapeDtypeStruct((b, t, n), F32),
        grid=(b // bb, t // tt, n // tn),
        in_specs=[pl.BlockSpec((bb, tt, k), row),
                  pl.BlockSpec((bb, tt, k), row),
                  pl.BlockSpec((bb, tt, k), row),
                  pl.BlockSpec((k, tn), lambda i, s, j: (0, j)),
                  pl.BlockSpec((bb, tt, tn), lambda i, s, j: (i, s, j)),
                  pl.BlockSpec((bb, 1, tn), lambda i, s, j: (i, 0, j))],
        out_specs=pl.BlockSpec((bb, tt, tn), lambda i, s, j: (i, s, j)),
        scratch_shapes=[pltpu.VMEM((bb * tt, k), BF16)],
        compiler_params=_cparams(("arbitrary", "arbitrary", "arbitrary"), 40),
        name="merge_out",
    )(oa, ga, ogb, w, x, gt)


def _ffn_kernel(h_ref, wg_ref, wu_ref, wd_ref, cw_ref, cb_ref, st_ref, y_ref, tail_ref, carry):
    s = pl.program_id(1)
    f = pl.program_id(2)
    bb, tt, d = h_ref.shape
    tm = bb * tt
    tf = wg_ref.shape[1]

    h = h_ref[...].reshape(tm, d)
    gate = jnp.dot(h, wg_ref[...], preferred_element_type=F32)
    up = jnp.dot(h, wu_ref[...], preferred_element_type=F32)

    st = st_ref[...]
    first = s == 0

    @pl.when(first)
    def _():
        carry[f] = jnp.zeros((SUBLANES, tf), F32)

    prev = carry[f]
    p0 = jnp.where(first, st[:, 0:1, :], prev[6:7, :][None])
    p1 = jnp.where(first, st[:, 1:2, :], prev[7:8, :][None])
    p0 = jnp.broadcast_to(p0, (bb, tt, tf)).reshape(tm, tf)
    p1 = jnp.broadcast_to(p1, (bb, tt, tf)).reshape(tm, tf)
    tpos = lax.broadcasted_iota(I32, (tm, tf), 0) % tt
    r1 = pltpu.roll(gate, 1, axis=0)
    r2 = pltpu.roll(gate, 2, axis=0)
    x1 = jnp.where(tpos == 0, p1, r1)
    x2 = jnp.where(tpos == 0, p0, jnp.where(tpos == 1, p1, r2))
    cw = cw_ref[...]
    a = cw[0:1] * x2 + cw[1:2] * x1 + cw[2:3] * gate + cb_ref[...]
    act = (_silu(a) * up).astype(BF16)
    contrib = jnp.dot(act, wd_ref[...], preferred_element_type=F32).reshape(bb, tt, d)

    g3 = gate.reshape(bb, tt, tf)
    tail_ref[...] = g3[:, None, tt - SUBLANES:, :]
    carry[f] = gate[tm - SUBLANES:, :]

    @pl.when(f == 0)
    def _():
        y_ref[...] = contrib

    @pl.when(f > 0)
    def _():
        y_ref[...] += contrib


def _ffn(h2, wg, wu, wd, cw, cb, st):
    b, t, d = h2.shape
    dff = wg.shape[1]
    bb, tt = _row_tiles(b, t, FFN_ROWS)
    tf = FFN_COLS
    nf = dff // tf
    row = lambda i, s, f: (i, s, 0)
    return pl.pallas_call(
        _ffn_kernel,
        out_shape=(jax.ShapeDtypeStruct((b, t, d), F32),
                   jax.ShapeDtypeStruct((b, t // tt, SUBLANES, dff), F32)),
        grid=(b // bb, t // tt, nf),
        in_specs=[pl.BlockSpec((bb, tt, d), row),
                  pl.BlockSpec((d, tf), lambda i, s, f: (0, f)),
                  pl.BlockSpec((d, tf), lambda i, s, f: (0, f)),
                  pl.BlockSpec((tf, d), lambda i, s, f: (f, 0)),
                  pl.BlockSpec((FFN_CONV_WIDTH, tf), lambda i, s, f: (0, f)),
                  pl.BlockSpec((1, tf), lambda i, s, f: (0, f)),
                  pl.BlockSpec((bb, FFN_CONV_WIDTH - 1, tf), lambda i, s, f: (i, 0, f))],
        out_specs=(pl.BlockSpec((bb, tt, d), row),
                   pl.BlockSpec((bb, 1, SUBLANES, tf), lambda i, s, f: (i, s, 0, f))),
        scratch_shapes=[pltpu.VMEM((nf, SUBLANES, tf), F32)],
        compiler_params=_cparams(("arbitrary", "arbitrary", "arbitrary"), 56),
        name="conv_ffn",
    )(h2, wg, wu, wd, cw, cb, st)


def _res_norm_kernel(x_ref, y_ref, gt_ref, gf_ref, o_ref):
    x = x_ref[...] + gt_ref[...] * y_ref[...]
    o_ref[...] = x * lax.rsqrt(jnp.mean(x * x, axis=-1, keepdims=True) + EPS) * gf_ref[...]


def _res_norm(x1, y, gt, gf):
    b, t, d = x1.shape
    bb, tt = _row_tiles(b, t, ELEMENTWISE_ROWS)
    row = lambda i, j: (i, j, 0)
    return pl.pallas_call(
        _res_norm_kernel,
        out_shape=jax.ShapeDtypeStruct((b, t, d), F32),
        grid=(b // bb, t // tt),
        in_specs=[pl.BlockSpec((bb, tt, d), row),
                  pl.BlockSpec((bb, tt, d), row),
                  pl.BlockSpec((bb, 1, d), lambda i, j: (i, 0, 0)),
                  pl.BlockSpec((1, d), lambda i, j: (0, 0))],
        out_specs=pl.BlockSpec((bb, tt, d), row),
        compiler_params=_cparams(("arbitrary", "arbitrary"), 40),
        name="res_norm",
    )(x1, y, gt, gf)


HEADS_PER_GROUP = 4
GROUP_ROWS = HEADS_PER_GROUP * CHUNK


def _delta_kernel(x_ref, st_ref, cw_ref, ab_ref, hp_ref, s0_ref, z_ref, gb_ref, gdn_ref,
                  o_ref, s_ref, carry, *, nheads):
    c = pl.program_id(1)
    r = GROUP_ROWS
    hd = DK_B

    @pl.when(c == 0)
    def _():
        carry[...] = st_ref[0]
        s_ref[...] = s0_ref[...]

    prev = carry[...]
    carry[...] = x_ref[0, CHUNK - SUBLANES:, :]

    ri = lax.broadcasted_iota(I32, (r, r), 0)
    ci = lax.broadcasted_iota(I32, (r, r), 1)
    same_head = (ri // CHUNK) == (ci // CHUNK)
    incl = same_head & (ri >= ci)
    strict = same_head & (ri > ci)
    triu = jnp.where(same_head & (ri <= ci), 1.0, 0.0).astype(F32)
    eye = jnp.where(ri == ci, 1.0, 0.0).astype(F32)
    row8 = lax.broadcasted_iota(I32, (SUBLANES, r * 2), 0)
    rid = lax.broadcasted_iota(I32, (LANES, r), 0)

    def conv_silu(col0, width):
        xs = x_ref[0, :, col0:col0 + width]
        pv = prev[:, col0:col0 + width]
        w = cw_ref[:, col0:col0 + width]
        y = xs * w[CONVB_WIDTH - 1:CONVB_WIDTH]
        for k in range(1, CONVB_WIDTH):
            rolled = pltpu.roll(xs, k, axis=0)
            head = jnp.where(row8[:, :width] < k, pltpu.roll(pv, k, axis=0), rolled[:SUBLANES])
            shifted = jnp.concatenate([head, rolled[SUBLANES:]], axis=0)
            y = y + shifted * w[CONVB_WIDTH - 1 - k:CONVB_WIDTH - k]
        return _silu(y)

    def l2n(v):
        return v * lax.rsqrt(jnp.sum(v * v, axis=-1, keepdims=True) + EPS)

    def stack(y, fn):
        return jnp.concatenate([fn(y[:, j * hd:(j + 1) * hd]) for j in range(HEADS_PER_GROUP)], axis=0)

    ngroups = nheads // HEADS_PER_GROUP
    gw = HEADS_PER_GROUP * hd

    def prep(g):
        qs = stack(conv_silu(g * gw, gw), lambda v: l2n(v) * (DK_B ** -0.5))
        ks = stack(conv_silu(nheads * hd + g * gw, gw), l2n)
        vs = stack(conv_silu(2 * nheads * hd + g * gw, gw), lambda v: v)
        a_row = ab_ref[0, 0, 0:1, g * r:(g + 1) * r]
        b_row = ab_ref[0, 0, 1:2, g * r:(g + 1) * r]
        alog = hp_ref[0:1, g * r:(g + 1) * r]
        dtb = hp_ref[1:2, g * r:(g + 1) * r]
        xsp = a_row + dtb
        softplus = jnp.maximum(xsp, 0.0) + jnp.log1p(jnp.exp(-jnp.abs(xsp)))
        g_row = -jnp.exp(alog) * softplus
        beta_row = _sigmoid(b_row)
        gc_row = _dot(jnp.broadcast_to(g_row, (SUBLANES, r)), triu, exact=True)[0:1]
        rows = jnp.where(rid == 0, beta_row, jnp.where(rid == 1, gc_row, 0.0))
        cols = rows.T
        beta_col = cols[:, 0:1]
        gc_col = cols[:, 1:2]
        gl_col = jnp.concatenate(
            [jnp.broadcast_to(cols[(j + 1) * CHUNK - 1:(j + 1) * CHUNK, 1:2], (CHUNK, 1))
             for j in range(HEADS_PER_GROUP)], axis=0)
        decay = jnp.where(incl, jnp.exp(gc_col - gc_row), 0.0)
        egc = jnp.exp(gc_col)
        kb = ks * beta_col
        return dict(qs=qs, ks=ks, kb=kb, rhs=jnp.concatenate([vs * beta_col, kb * egc], axis=1),
                    decay=decay, qg=qs * egc, kd=ks * jnp.exp(gl_col - gc_col), gl_col=gl_col)

    gs = [prep(g) for g in range(ngroups)]
    ms = [jnp.where(strict, _dot(d["kb"], d["ks"], _NT) * d["decay"], 0.0) for d in gs]

    invs = [eye - jnp.where((ri // 2 == ci // 2), m, 0.0) for m in ms]
    bs = 2
    while bs < CHUNK:
        lower = ((ri // (2 * bs)) == (ci // (2 * bs))) & ((ri % (2 * bs)) >= bs) & ((ci % (2 * bs)) < bs)
        xc = [_dot(inv, jnp.where(lower, m, 0.0)) for inv, m in zip(invs, ms)]
        invs = [inv - _dot(t, inv) for inv, t in zip(invs, xc)]
        bs *= 2

    uws = [_dot(inv, d["rhs"]) for inv, d in zip(invs, gs)]
    attns = [jnp.where(incl, _dot(d["qs"], d["ks"], _NT) * d["decay"], 0.0) for d in gs]

    vnews, qstates = [], []
    for g in range(ngroups):
        vnew, qstate = [], []
        for j in range(HEADS_PER_GROUP):
            sl = slice(j * CHUNK, (j + 1) * CHUNK)
            wq = jnp.concatenate([uws[g][sl, hd:], gs[g]["qg"][sl]], axis=0)
            rs = _dot(wq, s_ref[0, g * HEADS_PER_GROUP + j])
            vnew.append(uws[g][sl, :hd] - rs[:CHUNK])
            qstate.append(rs[CHUNK:])
        vnews.append(vnew)
        qstates.append(qstate)
    intras = [_dot(attns[g], jnp.concatenate(vnews[g], axis=0)) for g in range(ngroups)]

    for g in range(ngroups):
        for j in range(HEADS_PER_GROUP):
            hidx = g * HEADS_PER_GROUP + j
            sl = slice(j * CHUNK, (j + 1) * CHUNK)
            hs = slice(hidx * hd, (hidx + 1) * hd)
            o = qstates[g][j] + intras[g][sl]
            glast = jnp.exp(gs[g]["gl_col"][j * CHUNK:j * CHUNK + 1, :])
            upd = _dot(gs[g]["kd"][sl], vnews[g][j], _TN)
            s_ref[0, hidx] = s_ref[0, hidx] * glast + upd
            on = o * lax.rsqrt(jnp.mean(o * o, axis=-1, keepdims=True) + EPS) * gdn_ref[...]
            o_ref[0, :, hs] = on * _silu(z_ref[0, :, hs]) * _sigmoid(gb_ref[0, :, hs])


def _delta(xqkv, st8, cw, ab, hp, s0, z, gb, gdn):
    b, t, cdim = xqkv.shape
    nheads = s0.shape[1]
    dm = z.shape[2]
    n = t // CHUNK
    kern = functools.partial(_delta_kernel, nheads=nheads)
    return pl.pallas_call(
        kern,
        out_shape=(jax.ShapeDtypeStruct((b, t, dm), F32),
                   jax.ShapeDtypeStruct(s0.shape, F32)),
        grid=(b, n),
        in_specs=[pl.BlockSpec((1, CHUNK, cdim), lambda i, c: (i, c, 0)),
                  pl.BlockSpec((1, SUBLANES, cdim), lambda i, c: (i, 0, 0)),
                  pl.BlockSpec((CONVB_WIDTH, cdim), lambda i, c: (0, 0)),
                  pl.BlockSpec((1, 1, SUBLANES, nheads * CHUNK), lambda i, c: (i, c, 0, 0)),
                  pl.BlockSpec((SUBLANES, nheads * CHUNK), lambda i, c: (0, 0)),
                  pl.BlockSpec((1, nheads, DK_B, DV_B), lambda i, c: (i, 0, 0, 0)),
                  pl.BlockSpec((1, CHUNK, dm), lambda i, c: (i, c, 0)),
                  pl.BlockSpec((1, CHUNK, dm), lambda i, c: (i, c, 0)),
                  pl.BlockSpec((1, DV_B), lambda i, c: (0, 0))],
        out_specs=(pl.BlockSpec((1, CHUNK, dm), lambda i, c: (i, c, 0)),
                   pl.BlockSpec((1, nheads, DK_B, DV_B), lambda i, c: (i, 0, 0, 0))),
        scratch_shapes=[pltpu.VMEM((SUBLANES, cdim), F32)],
        compiler_params=_cparams(("arbitrary", "arbitrary"), 40),
        name="gated_delta",
    )(xqkv, st8, cw, ab, hp, s0, z, gb, gdn)


_T5_THRESHOLDS = tuple(math.ceil(8 * 2 ** (k / 2)) for k in range(1, 8))
_FAR_BUCKET = NUM_BUCKETS // 2 - 1


MASKED = -1e30
TOPK_CANDIDATES = 12
KEY_TILE = 256
FLASH_ROWS = 512


def _topk_kernel(qi_ref, sm_ref, kit_ref, mask_ref, skey, cand, thr, jcut, qis, wib,
                 *, past, qb, tk, rb, ncand, n_sel, nkt):
    i = pl.program_id(1)
    rep = tk // LANES
    kd = past // tk + i * (qb // tk)
    q0 = past + i * qb

    def sortable(x):
        bits = pltpu.bitcast(x, I32)
        return bits ^ ((bits >> 31) & 0x7FFFFFFF)

    def lanes(v):
        return jnp.concatenate([v] * rep, axis=1) if rep > 1 else v

    def select_threshold():
        sm = sm_ref[0]
        for h in range(H_I):
            qis[h] = qi_ref[0, :, h * D_IDX:(h + 1) * D_IDX]
            wcol = sm[:, D_IDX + h:D_IDX + h + 1] * ((H_I * D_IDX) ** -0.5)
            wib[h] = jnp.broadcast_to(wcol, (qb, LANES))

        qpos = q0 + lax.broadcasted_iota(I32, (qb, tk), 0)
        lim = (qpos // CHUNK + 1) * CHUNK
        lane = lax.broadcasted_iota(I32, (qb, tk), 1)

        def score_tile(j, _, diagonal):
            off = pl.multiple_of(j * tk, tk)
            kit = kit_ref[0, :, pl.ds(off, tk)]
            sc = jnp.zeros((qb, tk), F32)
            for h in range(H_I):
                d = jnp.dot(qis[h], kit, preferred_element_type=F32)
                sc = sc + jnp.maximum(d, 0.0) * lanes(wib[h])
            sc = sc + 0.0
            key = sortable(sc)
            if diagonal:
                ok = lane + off < lim
                key = jnp.where(ok, key, INT_MIN)
                sc = jnp.where(ok, sc, -jnp.inf)
            skey[:, pl.ds(off, tk)] = key
            for u in range(rep):
                x = sc[:, u * LANES:(u + 1) * LANES]
                for c in range(ncand):
                    a = cand[c]
                    cand[c] = jnp.maximum(a, x)
                    x = jnp.minimum(a, x)
            return 0

        cand[...] = jnp.full(cand.shape, -jnp.inf, F32)
        lax.fori_loop(0, kd, functools.partial(score_tile, diagonal=False), 0)
        score_tile(kd, 0, diagonal=True)
        for c in range(ncand):
            a = cand[c]
            cand[c] = pltpu.bitcast(jnp.where(a == -jnp.inf, INT_MIN, sortable(a)), F32)

        def kth_largest(count_ge, rows):
            def bit_body(bi, t):
                cnd = t + lax.shift_left(jnp.int32(1), jnp.int32(31) - bi)
                return jnp.where(count_ge(cnd) >= n_sel, cnd, t)
            return lax.fori_loop(0, 32, bit_body, jnp.full((rows, LANES), INT_MIN, I32))

        def row_total(c):
            return jnp.broadcast_to(jnp.sum(c, axis=1, keepdims=True), c.shape)

        def cand_count_ge(cnd):
            c = jnp.zeros((qb, LANES), I32)
            for k in range(ncand):
                c = c + jnp.where(pltpu.bitcast(cand[k], I32) >= cnd, 1, 0)
            return row_total(c)

        thr[...] = kth_largest(cand_count_ge, qb)

        def finish_rows(rblk, bad, full_search):
            r0 = pl.multiple_of(rblk * rb, rb)
            rlane = lax.broadcasted_iota(I32, (rb, tk), 1)

            def count(pred):
                def body(jj, c):
                    off = pl.multiple_of(jj * tk, tk)
                    mk = jnp.where(pred(skey[pl.ds(r0, rb), pl.ds(off, tk)], off), 1, 0)
                    part = mk[:, :LANES]
                    for u in range(1, rep):
                        part = part + mk[:, u * LANES:(u + 1) * LANES]
                    return c + part
                return row_total(lax.fori_loop(0, kd + 1, body, jnp.zeros((rb, LANES), I32)))

            if full_search:
                t = kth_largest(lambda cnd: count(lambda x, off: x >= lanes(cnd)), rb)
            else:
                t = thr[pl.ds(r0, rb), :]
            ts = jnp.maximum(t, INT_MIN + 1)
            thr[pl.ds(r0, rb), :] = ts
            jcut[pl.ds(r0, rb), :] = jnp.full((rb, LANES), 2 ** 30, I32)

            tsl = lanes(ts)
            c_gt = count(lambda x, off: x > tsl)
            c_eq = count(lambda x, off: x == tsl)
            need = n_sel - c_gt
            excess = jnp.max(jnp.where((c_eq > need) & (t != INT_MIN), 1.0, 0.0))

            @pl.when(excess > 0.5)
            def _():
                def jbit(bi, jc):
                    cnd = jc | lax.shift_left(jnp.int32(1), jnp.int32(15) - bi)
                    cl = lanes(cnd)
                    cnt = count(lambda x, off: (x == tsl) & (rlane + off < cl))
                    return jnp.where(cnt <= need, cnd, jc)
                jcut[pl.ds(r0, rb), :] = lax.fori_loop(0, 16, jbit, jnp.zeros((rb, LANES), I32))

            return jnp.maximum(bad, jnp.max(jnp.where(need <= 0, 1.0, 0.0)))

        bad = lax.fori_loop(0, qb // rb, functools.partial(finish_rows, full_search=False), jnp.float32(0.0))

        @pl.when(bad > 0.5)
        def _():
            lax.fori_loop(0, qb // rb, functools.partial(finish_rows, full_search=True), jnp.float32(0.0))

    select_threshold()

    ts = lanes(thr[...])
    jc = lanes(jcut[...])
    lane = lax.broadcasted_iota(I32, (qb, tk), 1)

    def mask_tile(jj, _):
        off = pl.multiple_of(jj * tk, tk)
        x = skey[:, pl.ds(off, tk)]
        sel = (x > ts) | ((x == ts) & (lane + off < jc))
        mask_ref[0, :, pl.ds(off, tk)] = jnp.where(sel, 0.0, MASKED).astype(mask_ref.dtype)
        return 0

    def blank_tile(jj, _):
        off = pl.multiple_of(jj * tk, tk)
        mask_ref[0, :, pl.ds(off, tk)] = jnp.full((qb, tk), MASKED, mask_ref.dtype)
        return 0

    lax.fori_loop(0, kd + 1, mask_tile, 0)
    lax.fori_loop(kd + 1, nkt, blank_tile, 0)


def _topk_mask(qi, small, kit, *, past, n_sel):
    b, t, _ = qi.shape
    lp = kit.shape[2]
    tk = KEY_TILE
    qb = min(t, tk)
    assert past % tk == 0 and lp % tk == 0 and t % qb == 0
    assert qb == tk or t == qb
    assert lp < 2 ** 16
    kern = functools.partial(_topk_kernel, past=past, qb=qb, tk=tk, rb=min(qb, 128), ncand=TOPK_CANDIDATES,
                             n_sel=n_sel, nkt=lp // tk)
    row = lambda bi, i: (bi, i, 0)
    return pl.pallas_call(
        kern,
        out_shape=jax.ShapeDtypeStruct((b, t, lp), BF16),
        grid=(b, t // qb),
        in_specs=[pl.BlockSpec((1, qb, H_I * D_IDX), row),
                  pl.BlockSpec((1, qb, LANES), row),
                  pl.BlockSpec((1, D_IDX, lp), lambda bi, i: (bi, 0, 0), pipeline_mode=pl.Buffered(1))],
        out_specs=pl.BlockSpec((1, qb, lp), row),
        scratch_shapes=[pltpu.VMEM((qb, lp), I32),
                        pltpu.VMEM((TOPK_CANDIDATES, qb, LANES), F32),
                        pltpu.VMEM((qb, LANES), I32),
                        pltpu.VMEM((qb, LANES), I32),
                        pltpu.VMEM((H_I, qb, D_IDX), BF16),
                        pltpu.VMEM((H_I, qb, LANES), F32)],
        compiler_params=_cparams(("arbitrary", "arbitrary"), 56),
        name="topk_mask",
    )(qi, small, kit)


def _flash_kernel(itab, jtab, rb_ref, q_ref, ma_ref, mb_ref, ka_ref, kb_ref, va_ref, vb_ref, o_ref,
                  madd, m_scr, l_scr, acc, dtile, *, past, qf, tk, nheads):
    b = pl.program_id(0)
    step = pl.program_id(1)
    i = itab[step]
    j = jtab[step]
    hd = HD_A
    rs = min(qf, tk)
    nsub = qf // rs
    kd_last = past // tk + (i * qf + qf - 1) // tk
    blk_b = kd_last - 2 * j
    blk_a = blk_b - 1

    @pl.when((b == 0) & (step == 0))
    def _():
        rr = lax.broadcasted_iota(I32, (rs, tk), 0)
        cc = lax.broadcasted_iota(I32, (rs, tk), 1)
        for slot in (0, 1):
            rel = cc - rr - slot * tk
            n = jnp.abs(rel)
            large = jnp.full((rs, tk), NUM_BUCKETS // 4, I32)
            for th in _T5_THRESHOLDS:
                large = large + jnp.where(n >= th, 1, 0)
            bucket = jnp.where(rel > 0, NUM_BUCKETS // 2, 0) + jnp.where(n < NUM_BUCKETS // 4, n, large)

            def head_body(h, _, bucket=bucket, slot=slot):
                base = rb_ref[_FAR_BUCKET, h]
                tile = jnp.zeros((rs, tk), F32)
                for bk in range(NUM_BUCKETS):
                    if bk != _FAR_BUCKET:
                        tile = jnp.where(bucket == bk, (rb_ref[bk, h] - base) * LOG2E, tile)
                dtile[h, slot] = tile
                return 0

            lax.fori_loop(0, nheads, head_body, 0)

    @pl.when(j == 0)
    def _():
        m_scr[...] = jnp.full(m_scr.shape, MASKED, F32)
        l_scr[...] = jnp.zeros(l_scr.shape, F32)
        acc[...] = jnp.zeros(acc.shape, F32)

    hang = jnp.where(blk_a < 0, MASKED, 0.0)
    madd[:, :tk] = ma_ref[0].astype(F32) + hang
    madd[:, tk:] = mb_ref[0].astype(F32)

    def bias(h, sub, blk):
        back = kd_last - (nsub - 1 - sub) - blk
        on = jnp.where((back == 0) | (back == 1), 1.0, 0.0)
        return dtile[h, jnp.clip(back, 0, 1)] * on

    def heads(near):
        for h in range(0, nheads, 2):
            pa, alpha_a = head(h, near)
            pc, alpha_c = head(h + 1, near)
            pair = slice(h * hd, (h + 2) * hd)
            lhs = jnp.concatenate([pa, pc], axis=0)
            pv = (jnp.dot(lhs[:, :tk], va_ref[0, :, pair], preferred_element_type=F32)
                  + jnp.dot(lhs[:, tk:], vb_ref[0, :, pair], preferred_element_type=F32))
            hs = slice(h * hd, (h + 1) * hd)
            acc[:, hs] = alpha_a * acc[:, hs] + pv[:qf, :hd]
            hs = slice((h + 1) * hd, (h + 2) * hd)
            acc[:, hs] = alpha_c * acc[:, hs] + pv[qf:, hd:]

    def head(h, near):
        hs = slice(h * hd, (h + 1) * hd)
        qh = q_ref[0, :, hs]
        sa = lax.dot_general(qh, ka_ref[0, :, hs], _NT, preferred_element_type=F32)
        sb = lax.dot_general(qh, kb_ref[0, :, hs], _NT, preferred_element_type=F32)
        if near:
            rows = []
            for sub in range(nsub):
                r = slice(sub * rs, (sub + 1) * rs)
                rows.append(jnp.concatenate([sa[r] + bias(h, sub, blk_a), sb[r] + bias(h, sub, blk_b)], axis=1))
            s = jnp.concatenate(rows, axis=0) if nsub > 1 else rows[0]
        else:
            s = jnp.concatenate([sa, sb], axis=1)
        s = s + madd[...]
        m_prev = m_scr[h]
        m_new = jnp.maximum(m_prev, jnp.max(s, axis=1, keepdims=True))
        alpha = jnp.exp2(m_prev - m_new)
        p = jnp.exp2(s - jnp.concatenate([m_new] * (2 * tk // LANES), axis=1))
        l_scr[h] = alpha * l_scr[h] + jnp.sum(p, axis=1, keepdims=True)
        m_scr[h] = m_new
        return p.astype(BF16), alpha

    near_steps = nsub // 2 + 1

    @pl.when(j < near_steps)
    def _():
        heads(True)

    @pl.when(j >= near_steps)
    def _():
        heads(False)

    @pl.when(j == kd_last // 2)
    def _():
        for h in range(nheads):
            hs = slice(h * hd, (h + 1) * hd)
            o_ref[0, :, hs] = (acc[:, hs] / l_scr[h]).astype(o_ref.dtype)


def _flash(rel_bias, q, mask, kall, vall, *, past):
    b, t, dm = q.shape
    lp = kall.shape[1]
    nheads = dm // HD_A
    tk = KEY_TILE
    qf = min(t, FLASH_ROWS)
    assert past % tk == 0 and lp % tk == 0 and t % qf == 0
    assert qf % tk == 0 or t == qf
    klast = lambda i: past // tk + (i * qf + qf - 1) // tk
    steps = [(i, j) for i in range(t // qf) for j in range(klast(i) // 2 + 1)]
    itab = jnp.asarray([s[0] for s in steps], I32)
    jtab = jnp.asarray([s[1] for s in steps], I32)
    kern = functools.partial(_flash_kernel, past=past, qf=qf, tk=tk, nheads=nheads)
    rs = min(qf, tk)

    def row(bi, s, it, jt):
        return (bi, it[s], 0)

    def tile_a(bi, s, it, jt):
        return (bi, jnp.maximum(klast(it[s]) - 2 * jt[s] - 1, 0), 0)

    def tile_b(bi, s, it, jt):
        return (bi, klast(it[s]) - 2 * jt[s], 0)

    def mask_a(bi, s, it, jt):
        return (bi, it[s], jnp.maximum(klast(it[s]) - 2 * jt[s] - 1, 0))

    def mask_b(bi, s, it, jt):
        return (bi, it[s], klast(it[s]) - 2 * jt[s])

    grid_spec = pltpu.PrefetchScalarGridSpec(
        num_scalar_prefetch=2,
        grid=(b, len(steps)),
        in_specs=[pl.BlockSpec(memory_space=pltpu.SMEM),
                  pl.BlockSpec((1, qf, dm), row),
                  pl.BlockSpec((1, qf, tk), mask_a),
                  pl.BlockSpec((1, qf, tk), mask_b),
                  pl.BlockSpec((1, tk, dm), tile_a),
                  pl.BlockSpec((1, tk, dm), tile_b),
                  pl.BlockSpec((1, tk, dm), tile_a),
                  pl.BlockSpec((1, tk, dm), tile_b)],
        out_specs=pl.BlockSpec((1, qf, dm), row),
        scratch_shapes=[pltpu.VMEM((qf, 2 * tk), F32),
                        pltpu.VMEM((nheads, qf, LANES), F32),
                        pltpu.VMEM((nheads, qf, LANES), F32),
                        pltpu.VMEM((qf, dm), F32),
                        pltpu.VMEM((nheads, 2, rs, tk), F32)])
    return pl.pallas_call(
        kern,
        out_shape=jax.ShapeDtypeStruct((b, t, dm), BF16),
        grid_spec=grid_spec,
        compiler_params=_cparams(("arbitrary", "arbitrary"), 56),
        name="masked_flash",
    )(itab, jtab, rel_bias, q, mask, mask, kall, kall, vall, vall)


def _pad_rows(a, rows):
    return jnp.pad(a, [(0, 0), (rows - a.shape[1], 0)] + [(0, 0)] * (a.ndim - 2))


def _run_group(x, mod, cache_k, cache_v, cache_kidx, st_conv, st_delta, st_ffn, rel_bias, g_final, w):
    b, t, d = x.shape
    nheads = d // HD_A
    sh1, sc1, gt1, sh2, sc2, gt2 = [m[:, None, :] for m in jnp.split(mod, 6, axis=-1)]
    m_rows = b * t

    h = _norm_mod(x, w["g_norm1"], sc1, sh1).reshape(m_rows, d)
    def proj(name, *dts):
        outs = _matmul(h, w[name], dts or (F32,), name="proj_" + name)
        outs = [o.reshape(b, t, -1) for o in outs]
        return outs if len(outs) > 1 else outs[0]

    qa, qi = proj("w_qa", BF16), proj("w_qi", BF16)
    (ka, ka16), (va, va16) = proj("w_ka", F32, BF16), proj("w_va", F32, BF16)
    small = proj("w_small")
    qkv_b, z_b, gate_a, gate_b = proj("w_qkvb"), proj("w_z"), proj("w_ga"), proj("w_gb")
    ki = small[..., :D_IDX]

    past = 0 if cache_k is None else cache_k.shape[1]
    l_all = past + t
    tk = KEY_TILE
    lp = -(-l_all // tk) * tk
    if cache_k is None:
        k_all, v_all, ki_all = ka16, va16, ki
    else:
        k_all = jnp.concatenate([cache_k.reshape(b, past, d).astype(BF16), ka16], axis=1)
        v_all = jnp.concatenate([cache_v.reshape(b, past, d).astype(BF16), va16], axis=1)
        ki_all = jnp.concatenate([cache_kidx, ki], axis=1)
    padk = lambda a: jnp.pad(a, ((0, 0), (0, lp - l_all), (0, 0))).astype(BF16)
    k_all, v_all = padk(k_all), padk(v_all)
    kit = jnp.swapaxes(padk(ki_all), 1, 2)
    n_sel = min(TOPK_MAX, l_all // 4)
    mask = _topk_mask(qi, small, kit, past=past, n_sel=n_sel)
    o_a = _flash(rel_bias, qa, mask, k_all, v_all, past=past)

    n = t // CHUNK
    to_rows = lambda v: jnp.swapaxes(v.reshape(b, n, CHUNK, nheads), 2, 3).reshape(b, n, 1, nheads * CHUNK)
    ab = jnp.concatenate([to_rows(small[..., 80:96]), to_rows(small[..., 96:112]),
                          jnp.zeros((b, n, SUBLANES - 2, nheads * CHUNK), F32)], axis=2)
    ogb, new_delta = _delta(qkv_b, _pad_rows(st_conv, SUBLANES), w["conv_b_w"], ab, w["hp_rows"],
                            st_delta, z_b, gate_b, w["g_delta_norm"])
    assert t >= CONVB_WIDTH - 1
    new_conv = qkv_b[:, t - (CONVB_WIDTH - 1):]

    x1 = _merge_out(o_a, gate_a, ogb, w["w_out"], x, gt1)
    h2 = _norm_mod(x1, w["g_norm2"], sc2, sh2)
    ffn, tail = _ffn(h2, w["w_ffn_gate"], w["w_ffn_up"], w["w_ffn_down"], w["ffn_conv_w"], w["ffn_conv_b"], st_ffn)
    y = _res_norm(x1, ffn, gt2, g_final)
    new_ffn = tail[:, -1, SUBLANES - (FFN_CONV_WIDTH - 1):]
    kshape = (1, b, t, nheads, HD_A)
    return (y, ka.reshape(kshape), va.reshape(kshape), ki[None], new_conv[None], new_delta[None], new_ffn[None])


def kernel(x_prompt, x_sample, c_prompt, c_sample, cache_k, cache_v, cache_kidx, state_conv_b, state_delta,
           state_ffn_conv, rel_bias, g_final, w_ada, b_ada, g_norm1, w_in, conv_b_w, a_log, dt_bias,
           g_delta_norm, w_out, g_norm2, w_ffn_gate, ffn_conv_w, ffn_conv_b, w_ffn_up, w_ffn_down):
    assert w_ada.shape[0] == 1, "one layer"
    d = x_prompt.shape[-1]
    nheads = d // HD_A
    bp, bs = x_prompt.shape[0], x_sample.shape[0]
    convb = 3 * nheads * DK_B

    wi_ = w_in[0]
    sizes = (d, d, d, H_I * D_IDX, D_IDX, H_I, convb, nheads, nheads, d, d, d)
    offs = [0]
    for s in sizes:
        offs.append(offs[-1] + s)
    col = lambda k: wi_[:, offs[k]:offs[k + 1]]
    w_small = jnp.concatenate([col(4), col(5), col(7), col(8),
                               jnp.zeros((d, LANES - D_IDX - H_I - 2 * nheads), F32)], axis=1)
    bf = lambda a: a.astype(BF16)
    w = dict(
        w_qa=bf(col(0) * (HD_A ** -0.5 * LOG2E)), w_ka=bf(col(1)), w_va=bf(col(2)), w_qi=bf(col(3)), w_small=bf(w_small),
        w_qkvb=bf(col(6)), w_z=bf(col(9)), w_ga=bf(col(10)), w_gb=bf(col(11)),
        g_norm1=g_norm1, g_norm2=g_norm2, conv_b_w=conv_b_w[0], g_delta_norm=g_delta_norm,
        w_out=bf(w_out[0]), w_ffn_gate=bf(w_ffn_gate[0]), w_ffn_up=bf(w_ffn_up[0]),
        w_ffn_down=bf(w_ffn_down[0]), ffn_conv_w=ffn_conv_w[0], ffn_conv_b=ffn_conv_b,
        hp_rows=jnp.concatenate([jnp.repeat(a_log[0], CHUNK)[None], jnp.repeat(dt_bias[0], CHUNK)[None],
                                 jnp.zeros((SUBLANES - 2, nheads * CHUNK), F32)], axis=0),
    )

    c_all = jnp.concatenate([c_prompt, c_sample], axis=0)
    rows = -(-c_all.shape[0] // SUBLANES) * SUBLANES
    c_all = jnp.pad(c_all, ((0, rows - c_all.shape[0]), (0, 0)))
    mod = _adaln(c_all, w_ada[0], b_ada)

    zeros = lambda *s: jnp.zeros(s, F32)
    outs_p = _run_group(x_prompt, mod[:bp], None, None, None,
                        zeros(bp, CONVB_WIDTH - 1, convb), zeros(bp, nheads, DK_B, DV_B),
                        zeros(bp, FFN_CONV_WIDTH - 1, w_ffn_gate.shape[-1]), rel_bias, g_final[None], w)
    outs_s = _run_group(x_sample, mod[bp:bp + bs], cache_k[0], cache_v[0], cache_kidx[0],
                        state_conv_b[0], state_delta[0], state_ffn_conv[0], rel_bias, g_final[None], w)
    return (outs_p[0], outs_s[0]) + tuple(outs_p[1:]) + tuple(outs_s[1:])
```
